```python
import jax, jax.numpy as jnp
from jax import lax
import numpy as np

D_MODEL = 1024
BATCH = 8
SEQ = 4096
DEPTH = 4

GRID_W = 64
CTX_LEN = 256
N_MIXERS = 3
D_FF = 2816
N_MOD = 9
ALPHA = (2 * DEPTH) ** 0.25
BETA = (8 * DEPTH) ** -0.25
LN_EPS = 1e-5

ML_INNER = 2 * D_MODEL
ML_HEADS = 4
ML_HEAD_DIM = ML_INNER // ML_HEADS
ML_QKV_BLOCK = 4
ML_CHUNK = 64
ML_CONV = 3

AT_HEADS = 16
AT_KV_HEADS = 4
AT_HEAD_DIM = 64
AT_WINDOW = 128
AT_BLOCK = 128
AT_SIDE_BLOCKS = -(-AT_WINDOW // AT_BLOCK)
ROPE_BASE = 10000.0

SC_WIDTH = 3

N_A = (DEPTH + 2) // N_MIXERS
N_B = (DEPTH + 1) // N_MIXERS
N_C = DEPTH // N_MIXERS

kernel_name = 'hybrid_mlstm_swa_shortconv_flow_backbone'


def layer_norm(x, g, b):
    xf = x.astype(jnp.float32)
    mu = xf.mean(-1, keepdims=True)
    var = jnp.mean(jnp.square(xf - mu), -1, keepdims=True)
    return ((xf - mu) * lax.rsqrt(var + LN_EPS) * g.astype(jnp.float32) + b.astype(jnp.float32)).astype(x.dtype)


def modulate(s, m, slot):
    return s * (1 + m[..., 3 * slot + 1, :]) + m[..., 3 * slot, :]


def gate_of(m, slot):
    return m[..., 3 * slot + 2, :]


def residual_post_norm(s, y, gate, weight, g, b):
    return layer_norm(ALPHA * s + weight * gate * y, g, b)


def swiglu(u, w_in, w_out):
    g, v = jnp.split(u @ w_in, 2, axis=-1)
    return (jax.nn.silu(g) * v) @ w_out


def dwconv_centred(u, w):
    K, C = w.shape
    return lax.conv_general_dilated(u, w[:, None, :], window_strides=(1,), padding=[(K // 2, K // 2)],
                                    dimension_numbers=('NWC', 'WIO', 'NWC'), feature_group_count=C)


def blockdiag(u, w):
    nblk, bs, _ = w.shape
    return jnp.einsum('blnc,ncd->blnd', u.reshape(u.shape[:2] + (nblk, bs)), w).reshape(u.shape)


def axial_rope(n_tokens, head_dim):
    rows_n = n_tokens // GRID_W
    rows = jnp.repeat(jnp.arange(rows_n), GRID_W).astype(jnp.float32)
    cols = jnp.tile(jnp.arange(GRID_W), rows_n).astype(jnp.float32)
    axis_dim = head_dim // 2
    freqs = ROPE_BASE ** (-jnp.arange(0, axis_dim, 2, dtype=jnp.float32) / axis_dim)
    ang = jnp.concatenate([rows[:, None] * freqs, cols[:, None] * freqs], axis=-1)
    return jnp.cos(ang), jnp.sin(ang)


def apply_rope(x, cos, sin):
    xf = x.astype(jnp.float32).reshape(x.shape[:-1] + (-1, 2))
    x1, x2 = xf[..., 0], xf[..., 1]
    return jnp.stack([x1 * cos - x2 * sin, x1 * sin + x2 * cos], axis=-1).reshape(x.shape).astype(x.dtype)


def sink_attend(q, k, v, mask, sink):
    G, R = q.shape[2], q.shape[3]
    s = jnp.einsum('bqgrd,bkgd->bgrqk', q, k).astype(jnp.float32)
    if mask is not None:
        s = jnp.where(mask, s, -jnp.inf)
    sk = sink.astype(jnp.float32).reshape(1, G, R, 1, 1)
    mx = jnp.maximum(s.max(-1, keepdims=True), sk)
    p = jnp.exp(s - mx)
    den = p.sum(-1, keepdims=True) + jnp.exp(sk - mx)
    return jnp.einsum('bgrqk,bkgd->bqgrd', (p / den).astype(v.dtype), v)


def window_attention(ul, uc, w_qkv, sink, w_o, ctx_out):
    B, S, _ = ul.shape
    Lc = uc.shape[1]
    H, G, d = AT_HEADS, AT_KV_HEADS, AT_HEAD_DIM
    R = H // G
    P = AT_SIDE_BLOCKS * AT_BLOCK
    nb = S // AT_BLOCK
    KW = (2 * AT_SIDE_BLOCKS + 1) * AT_BLOCK

    def proj(u):
        L = u.shape[1]
        q, k, v = jnp.split(u @ w_qkv, [H * d, (H + G) * d], axis=-1)
        return q.reshape(B, L, G, R, d) * d ** -0.5, k.reshape(B, L, G, d), v.reshape(B, L, G, d)

    ql, kl, vl = proj(ul)
    qc, kc, vc = proj(uc)
    cos, sin = axial_rope(S, d)
    ql = apply_rope(ql, cos[:, None, None], sin[:, None, None])
    kl = apply_rope(kl, cos[:, None], sin[:, None])

    def band(a):
        ab = jnp.pad(a, ((0, 0), (P, P), (0, 0), (0, 0))).reshape(B, nb + 2 * AT_SIDE_BLOCKS, AT_BLOCK, G, d)
        win = jnp.concatenate([ab[:, o:o + nb] for o in range(2 * AT_SIDE_BLOCKS + 1)], axis=2)
        return jnp.moveaxis(win, 1, 0)

    qb = jnp.moveaxis(ql.reshape(B, nb, AT_BLOCK, G, R, d), 1, 0)
    q_pos = jnp.arange(nb)[:, None] * AT_BLOCK + jnp.arange(AT_BLOCK)
    k_pos = jnp.arange(nb)[:, None] * AT_BLOCK - P + jnp.arange(KW)
    valid = ((jnp.abs(q_pos[:, :, None] - k_pos[:, None, :]) <= AT_WINDOW)
             & (k_pos >= 0)[:, None, :] & (k_pos < S)[:, None, :])
    mask = jnp.concatenate([valid, jnp.ones((nb, AT_BLOCK, Lc), dtype=bool)], axis=-1)

    def attend_block(args):
        q, kw, vw, m = args
        return sink_attend(q, jnp.concatenate([kw, kc], axis=1), jnp.concatenate([vw, vc], axis=1), m, sink)

    ol = lax.map(attend_block, (qb, band(kl), band(vl), mask))
    yl = jnp.moveaxis(ol, 0, 1).reshape(B, S, H * d) @ w_o
    yc = sink_attend(qc, kc, vc, None, sink).reshape(B, Lc, H * d) @ w_o if ctx_out else None
    return yl, yc


def mlstm_chunked(q, k, v, ig, fg, state):
    B, H, L, dh = q.shape
    T = ML_CHUNK
    nc = L // T
    k = k * dh ** -0.5
    chunk = lambda a: jnp.moveaxis(a.reshape((B, H, nc, T) + a.shape[3:]), 2, 0)
    logf = jax.nn.log_sigmoid(fg)
    causal = jnp.tril(jnp.ones((T, T), dtype=bool))

    def step(carry, inp):
        C, n, m = carry
        qc, kc, vc, ic, lf = inp
        b = jnp.cumsum(lf, axis=-1)
        g = b[..., -1]
        d_intra = jnp.where(causal, b[..., :, None] - b[..., None, :] + ic[..., None, :], -jnp.inf)
        d_inter = b + m[..., None]
        m_t = jnp.maximum(d_inter, d_intra.max(-1))
        w_intra = jnp.einsum('bhtd,bhsd->bhts', qc, kc) * jnp.exp(d_intra - m_t[..., None])
        w_inter = jnp.exp(d_inter - m_t)
        num = jnp.einsum('bhts,bhse->bhte', w_intra, vc) + w_inter[..., None] * jnp.einsum('bhtd,bhde->bhte', qc, C)
        den = w_intra.sum(-1) + w_inter * jnp.einsum('bhtd,bhd->bht', qc, n)
        h = num / jnp.maximum(jnp.abs(den), jnp.exp(-m_t))[..., None]
        a = g[..., None] - b + ic
        m_new = jnp.maximum(g + m, a.max(-1))
        w_s = jnp.exp(a - m_new[..., None])
        decay = jnp.exp(g + m - m_new)
        C = decay[..., None, None] * C + jnp.einsum('bhsd,bhse->bhde', kc * w_s[..., None], vc)
        n = decay[..., None] * n + jnp.einsum('bhs,bhsd->bhd', w_s, kc)
        return (C, n, m_new), h

    state, hs = lax.scan(step, state, (chunk(q), chunk(k), chunk(v), chunk(ig), chunk(logf)))
    return state, jnp.moveaxis(hs, 0, 2).reshape(B, H, L, dh)


def mlstm_mixer(ul, uc, w_up, conv_w, conv_b, w_qkv, w_if, b_if, skip, norm_g, w_down, ctx_out):
    E, H, dh = ML_INNER, ML_HEADS, ML_HEAD_DIM
    B = ul.shape[0]

    def prep(u):
        L = u.shape[1]
        xm, z = jnp.split(u @ w_up, 2, axis=-1)
        xc = jax.nn.silu(dwconv_centred(xm, conv_w) + conv_b)
        q, k, v = blockdiag(xc, w_qkv[0]), blockdiag(xc, w_qkv[1]), blockdiag(xm, w_qkv[2])
        gates = (jnp.einsum('ble,xeg->xbgl', q, w_if[:, :E]) + jnp.einsum('ble,xeg->xbgl', k, w_if[:, E:2 * E])
                 + jnp.einsum('ble,xeg->xbgl', v, w_if[:, 2 * E:])).astype(jnp.float32) \
            + b_if.astype(jnp.float32)[:, None, :, None]
        heads = lambda a: a.reshape(B, L, H, dh).transpose(0, 2, 1, 3).astype(jnp.float32)
        return xc, z, heads(q), heads(k), heads(v), gates

    p_c, p_l = prep(uc), prep(ul)
    zero = (jnp.zeros((B, H, dh, dh), jnp.float32), jnp.zeros((B, H, dh), jnp.float32), jnp.zeros((B, H), jnp.float32))

    def scan_dir(p, direction, state):
        q, k, v, gates = p[2:]
        ig, fg = gates[direction, :, :H], gates[direction, :, H:]
        if direction == 1:
            q, k, v, ig, fg = [jnp.flip(a, axis=2) for a in (q, k, v, ig, fg)]
        state, h = mlstm_chunked(q, k, v, ig, fg, state)
        return state, (jnp.flip(h, axis=2) if direction == 1 else h)

    st_f, hc_f = scan_dir(p_c, 0, zero)
    st_b, hc_b = scan_dir(p_c, 1, zero)
    _, hl_f = scan_dir(p_l, 0, st_f)
    _, hl_b = scan_dir(p_l, 1, st_b)

    def finish(p, h):
        xc, z = p[0], p[1]
        L = xc.shape[1]
        h = jax.nn.sigmoid(z.astype(jnp.float32)).reshape(B, L, H, dh) * h.transpose(0, 2, 1, 3)
        mu = h.mean(-1, keepdims=True)
        var = jnp.mean(jnp.square(h - mu), -1, keepdims=True)
        h = ((h - mu) * lax.rsqrt(var + LN_EPS) * norm_g.astype(jnp.float32).reshape(H, dh)).reshape(B, L, E)
        return (h.astype(xc.dtype) + skip * xc) @ w_down

    return finish(p_l, hl_f + hl_b), (finish(p_c, hc_f + hc_b) if ctx_out else None)


def short_conv_mixer(u, w_in, conv_w, w_out):
    bg, cg, xt = jnp.split(u @ w_in, 3, axis=-1)
    return (bg * dwconv_centred(cg * xt, conv_w)) @ w_out


def setup_inputs(seed: int = 0) -> dict:
    key = jax.random.key(seed)
    ks = jax.random.split(key, 26)
    f32 = jnp.float32
    nrm = lambda k, shape, s: jax.random.normal(k, shape, f32) * s
    D, E, H, F = D_MODEL, ML_INNER, ML_HEADS, D_FF
    at_cols = (AT_HEADS + 2 * AT_KV_HEADS) * AT_HEAD_DIM
    ml_b_if = jnp.concatenate([nrm(ks[15], (N_A, 2, H), 0.1),
                               jnp.linspace(3.0, 6.0, H, dtype=f32) + nrm(ks[16], (N_A, 2, H), 0.1)], axis=-1)
    return {
        'x': nrm(ks[0], (BATCH, SEQ, D), 1.0),
        'c': nrm(ks[1], (BATCH, D), 1.0),
        'ctx': nrm(ks[2], (BATCH, CTX_LEN, D), 1.0),
        'c_ctx': nrm(ks[3], (D,), 1.0),
        'mod_w': nrm(ks[4], (DEPTH, D, N_MOD * D), 0.5 * D ** -0.5),
        'mod_b': nrm(ks[5], (DEPTH, N_MOD * D), 0.02),
        'ln_g': 1.0 + nrm(ks[6], (DEPTH, 3, D), 0.02),
        'ln_b': nrm(ks[7], (DEPTH, 3, D), 0.02),
        'ffn_w_in': nrm(ks[8], (DEPTH, 2, D, 2 * F), D ** -0.5),
        'ffn_w_out': nrm(ks[9], (DEPTH, 2, F, D), BETA * F ** -0.5),
        'ml_w_up': nrm(ks[10], (N_A, D, 2 * E), D ** -0.5),
        'ml_conv_w': nrm(ks[11], (N_A, ML_CONV, E), ML_CONV ** -0.5),
        'ml_conv_b': nrm(ks[12], (N_A, E), 0.02),
        'ml_w_qkv': nrm(ks[13], (N_A, 3, E // ML_QKV_BLOCK, ML_QKV_BLOCK, ML_QKV_BLOCK), ML_QKV_BLOCK ** -0.5),
        'ml_w_if': nrm(ks[14], (N_A, 2, 3 * E, 2 * H), 0.1 * (3 * E) ** -0.5),
        'ml_b_if': ml_b_if,
        'ml_skip': 1.0 + nrm(ks[17], (N_A, E), 0.02),
        'ml_norm_g': 1.0 + nrm(ks[18], (N_A, E), 0.02),
        'ml_w_down': nrm(ks[19], (N_A, E, D), BETA * E ** -0.5),
        'at_w_qkv': nrm(ks[20], (N_B, D, at_cols), D ** -0.5),
        'at_sink': nrm(ks[21], (N_B, AT_HEADS), 0.5),
        'at_w_o': nrm(ks[22], (N_B, AT_HEADS * AT_HEAD_DIM, D), BETA * (AT_HEADS * AT_HEAD_DIM) ** -0.5),
        'sc_w_in': nrm(ks[23], (N_C, D, 3 * D), D ** -0.5),
        'sc_conv_w': nrm(ks[24], (N_C, SC_WIDTH, D), SC_WIDTH ** -0.5),
        'sc_w_out': nrm(ks[25], (N_C, D, D), BETA * D ** -0.5),
    }


def reference(x, c, ctx, c_ctx, mod_w, mod_b, ln_g, ln_b, ffn_w_in, ffn_w_out,
              ml_w_up, ml_conv_w, ml_conv_b, ml_w_qkv, ml_w_if, ml_b_if, ml_skip, ml_norm_g, ml_w_down,
              at_w_qkv, at_sink, at_w_o, sc_w_in, sc_conv_w, sc_w_out):
    B = x.shape[0]
    lat_cond = jax.nn.silu(c)
    ctx_cond = jax.nn.silu(c_ctx)
    h, hc = x, ctx
    for i in range(DEPTH):
        kind, j, last = i % N_MIXERS, i // N_MIXERS, i == DEPTH - 1
        m_l = (lat_cond @ mod_w[i] + mod_b[i]).reshape(B, 1, N_MOD, D_MODEL)
        m_c = (ctx_cond @ mod_w[i] + mod_b[i]).reshape(N_MOD, D_MODEL)
        h = residual_post_norm(h, swiglu(modulate(h, m_l, 0), ffn_w_in[i, 0], ffn_w_out[i, 0]),
                               gate_of(m_l, 0), 0.5, ln_g[i, 0], ln_b[i, 0])
        hc = residual_post_norm(hc, swiglu(modulate(hc, m_c, 0), ffn_w_in[i, 0], ffn_w_out[i, 0]),
                                gate_of(m_c, 0), 0.5, ln_g[i, 0], ln_b[i, 0])
        ul, uc = modulate(h, m_l, 1), modulate(hc, m_c, 1)
        if kind == 0:
            yl, yc = mlstm_mixer(ul, uc, ml_w_up[j], ml_conv_w[j], ml_conv_b[j], ml_w_qkv[j], ml_w_if[j],
                                 ml_b_if[j], ml_skip[j], ml_norm_g[j], ml_w_down[j], not last)
        elif kind == 1:
            yl, yc = window_attention(ul, uc, at_w_qkv[j], at_sink[j], at_w_o[j], not last)
        else:
            yl = short_conv_mixer(ul, sc_w_in[j], sc_conv_w[j], sc_w_out[j])
            yc = None if last else short_conv_mixer(uc, sc_w_in[j], sc_conv_w[j], sc_w_out[j])
        h = residual_post_norm(h, yl, gate_of(m_l, 1), 1.0, ln_g[i, 1], ln_b[i, 1])
        h = residual_post_norm(h, swiglu(modulate(h, m_l, 2), ffn_w_in[i, 1], ffn_w_out[i, 1]),
                               gate_of(m_l, 2), 0.5, ln_g[i, 2], ln_b[i, 2])
        if not last:
            hc = residual_post_norm(hc, yc, gate_of(m_c, 1), 1.0, ln_g[i, 1], ln_b[i, 1])
            hc = residual_post_norm(hc, swiglu(modulate(hc, m_c, 2), ffn_w_in[i, 1], ffn_w_out[i, 1]),
                                    gate_of(m_c, 2), 0.5, ln_g[i, 2], ln_b[i, 2])
    return h
```

```python
import functools

import jax
import jax.numpy as jnp
import numpy as np
from jax import lax
from jax.experimental import pallas as pl
from jax.experimental.pallas import tpu as pltpu

N_MIXERS = 3
N_MOD = 9
LN_EPS = 1e-5
GRID_W = 64
ML_HEADS = 4
ML_QKV_BLOCK = 4
AT_HEADS = 16
AT_KV_HEADS = 4
AT_HEAD_DIM = 64
AT_WINDOW = 128
ROPE_BASE = 10000.0

V7X_LANES = 128
V7X_SUBLANES = 8
V7X_MXU_DIM = 256
V7X_VMEM_LIMIT_BYTES = 56 * 1024 * 1024

ROW_TILE = 256
ML_CHUNK = 256
MOD_ROWS = 16
MOD_COL_TILE = 1024

BF16 = jnp.bfloat16
F32 = jnp.float32


def _cparams(*sem):
    return pltpu.CompilerParams(dimension_semantics=sem, vmem_limit_bytes=V7X_VMEM_LIMIT_BYTES)


def _resident(shape):
    nd = len(shape)
    return pl.BlockSpec(shape, lambda *_: (0,) * nd, pipeline_mode=pl.Buffered(1))


def _dot(a, b):
    return jnp.dot(a, b, preferred_element_type=F32)


def _sigmoid(x):
    return 1.0 / (1.0 + jnp.exp(-x))


def _silu(x):
    return x * _sigmoid(x)


def _modulate(h, mod_ref, slot):
    shift = mod_ref[pl.ds(3 * slot, 1), :]
    scale = mod_ref[pl.ds(3 * slot + 1, 1), :]
    return h * (1.0 + scale) + shift


def _post_norm(h, y, gate, weight, g, b, alpha):
    z = alpha * h + (weight * gate) * y
    mu = jnp.mean(z, axis=-1, keepdims=True)
    zc = z - mu
    var = jnp.mean(zc * zc, axis=-1, keepdims=True)
    return zc * lax.rsqrt(var + LN_EPS) * g + b


def _mod_kernel(cond_ref, w_ref, b_ref, o_ref):
    cond = _silu(cond_ref[...]).astype(BF16)
    o_ref[...] = _dot(cond, w_ref[...].astype(BF16)) + b_ref[...]


def _mod_all(cond, mod_w, mod_b):
    depth, d, n = mod_w.shape
    tn = MOD_COL_TILE
    return pl.pallas_call(
        _mod_kernel,
        grid=(depth, n // tn),
        in_specs=[
            pl.BlockSpec((MOD_ROWS, d), lambda i, j: (0, 0)),
            pl.BlockSpec((None, d, tn), lambda i, j: (i, 0, j)),
            pl.BlockSpec((None, 1, tn), lambda i, j: (i, 0, j)),
        ],
        out_specs=pl.BlockSpec((None, MOD_ROWS, tn), lambda i, j: (i, 0, j)),
        out_shape=jax.ShapeDtypeStruct((depth, MOD_ROWS, n), F32),
        compiler_params=_cparams("parallel", "parallel"),
        name="mod_vectors",
    )(cond, mod_w, mod_b.reshape(depth, 1, n))


class _Stream:
    def __init__(self, rows, seq, mod_base, rows_per_mod):
        self.rows = rows
        self.seq = seq
        self.mod_base = mod_base
        self.rows_per_mod = rows_per_mod

    def mod_spec(self, tm, d):
        base, rpm = self.mod_base, self.rows_per_mod
        return pl.BlockSpec((None, N_MOD, d), lambda i: (base + (i * tm) // rpm, 0, 0))


def _row_spec(tm, width):
    return pl.BlockSpec((tm, width), lambda i: (i, 0))


def _ffn_kernel(h_ref, mod_ref, wg_ref, wv_ref, wo_ref, lng_ref, lnb_ref, o_ref, *, slot, alpha):
    h = h_ref[...]
    u = _modulate(h, mod_ref, slot).astype(BF16)
    g = _dot(u, wg_ref[...])
    v = _dot(u, wv_ref[...])
    a = (_silu(g) * v).astype(BF16)
    y = _dot(a, wo_ref[...])
    gate = mod_ref[pl.ds(3 * slot + 2, 1), :]
    o_ref[...] = _post_norm(h, y, gate, 0.5, lng_ref[...], lnb_ref[...], alpha)


def _ffn(h, st, mod, wg, wv, wo, lng, lnb, slot, alpha, tm):
    rows, d = h.shape
    f = wg.shape[1]
    return pl.pallas_call(
        functools.partial(_ffn_kernel, slot=slot, alpha=alpha),
        grid=(rows // tm,),
        in_specs=[
            _row_spec(tm, d),
            st.mod_spec(tm, d),
            _resident((d, f)),
            _resident((d, f)),
            _resident((f, d)),
            _resident((1, d)),
            _resident((1, d)),
        ],
        out_specs=_row_spec(tm, d),
        out_shape=jax.ShapeDtypeStruct((rows, d), F32),
        compiler_params=_cparams("parallel"),
        name="ffn_swiglu",
    )(h, mod, wg, wv, wo, lng, lnb)


def _halo_specs(tm, width, rows):
    r8 = tm // V7X_SUBLANES
    last8 = rows // V7X_SUBLANES - 1
    prev = pl.BlockSpec((V7X_SUBLANES, width), lambda i: (jnp.maximum(i * r8 - 1, 0), 0))
    nxt = pl.BlockSpec((V7X_SUBLANES, width), lambda i: (jnp.minimum((i + 1) * r8, last8), 0))
    return prev, nxt


def _halo_valid(tm, seq):
    tiles_per_seq = seq // tm
    pos = pl.program_id(0) % tiles_per_seq
    prev_ok = jnp.where(pos == 0, 0.0, 1.0).astype(F32)
    next_ok = jnp.where(pos == tiles_per_seq - 1, 0.0, 1.0).astype(F32)
    return prev_ok, next_ok


def _conv3(t_prev_row, t_main, t_next_row, w_ref, tm):
    row = lax.broadcasted_iota(jnp.int32, (tm, 1), 0)
    up = jnp.where(row == 0, t_prev_row, pltpu.roll(t_main, 1, axis=0))
    down = jnp.where(row == tm - 1, t_next_row, pltpu.roll(t_main, tm - 1, axis=0))
    return w_ref[0:1, :] * up + w_ref[1:2, :] * t_main + w_ref[2:3, :] * down


def _sconv_kernel(hp_ref, h_ref, hn_ref, mod_ref, wb_ref, wcx_ref, cw_ref, wo_ref, lng_ref, lnb_ref,
                  o_ref, *, alpha, tm, seq):
    d = h_ref.shape[1]
    h = h_ref[...]
    prev_ok, next_ok = _halo_valid(tm, seq)
    u_ext = _modulate(jnp.concatenate([hp_ref[...], h, hn_ref[...]], axis=0), mod_ref, 1).astype(BF16)
    lo, hi = V7X_SUBLANES, V7X_SUBLANES + tm
    bg = _dot(_modulate(h, mod_ref, 1).astype(BF16), wb_ref[...])
    cx = _dot(u_ext, wcx_ref[...])
    t_ext = cx[:, :d] * cx[:, d:]
    t_prev = t_ext[lo - 1:lo, :] * prev_ok
    t_next = t_ext[hi:hi + 1, :] * next_ok
    conv = _conv3(t_prev, t_ext[lo:hi], t_next, cw_ref, tm)
    y = _dot((bg * conv).astype(BF16), wo_ref[...])
    gate = mod_ref[pl.ds(5, 1), :]
    o_ref[...] = _post_norm(h, y, gate, 1.0, lng_ref[...], lnb_ref[...], alpha)


def _sconv(h, st, mod, wb, wcx, cw, wo, lng, lnb, alpha):
    rows, d = h.shape
    tm = ROW_TILE
    prev, nxt = _halo_specs(tm, d, rows)
    return pl.pallas_call(
        functools.partial(_sconv_kernel, alpha=alpha, tm=tm, seq=st.seq),
        grid=(rows // tm,),
        in_specs=[
            prev, _row_spec(tm, d), nxt,
            st.mod_spec(tm, d),
            _resident(wb.shape), _resident(wcx.shape), _resident(cw.shape), _resident(wo.shape),
            _resident((1, d)), _resident((1, d)),
        ],
        out_specs=_row_spec(tm, d),
        out_shape=jax.ShapeDtypeStruct((rows, d), F32),
        compiler_params=_cparams("parallel"),
        name="short_conv_mixer",
    )(h, h, h, mod, wb, wcx, cw, wo, lng, lnb)


def _rope_tables(n_tokens):
    rows_n = n_tokens // GRID_W
    rows = jnp.repeat(jnp.arange(rows_n), GRID_W).astype(F32)
    cols = jnp.tile(jnp.arange(GRID_W), rows_n).astype(F32)
    axis_dim = AT_HEAD_DIM // 2
    freqs = ROPE_BASE ** (-jnp.arange(0, axis_dim, 2, dtype=F32) / axis_dim)
    ang = jnp.concatenate([rows[:, None] * freqs, cols[:, None] * freqs], axis=-1)
    cos, sin = jnp.cos(ang), jnp.sin(ang)
    cos_t = jnp.tile(jnp.concatenate([cos, cos], axis=-1), (1, 2))
    sin_t = jnp.tile(jnp.concatenate([-sin, sin], axis=-1), (1, 2))
    return cos_t, sin_t


def _swap_halves(x):
    width = x.shape[1]
    half = AT_HEAD_DIM // 2
    lane = lax.broadcasted_iota(jnp.int32, (1, width), 1)
    lo = (lane % AT_HEAD_DIM) < half
    return jnp.where(lo, pltpu.roll(x, width - half, axis=1), pltpu.roll(x, half, axis=1))


def _qkv_kernel(*refs, rope):
    if rope:
        h_ref, mod_ref, w_ref, cos_ref, sin_ref, q_ref, k_ref, v_ref = refs
    else:
        h_ref, mod_ref, w_ref, q_ref, k_ref, v_ref = refs
    nq, nk = q_ref.shape[1], k_ref.shape[1]
    u = _modulate(h_ref[...], mod_ref, 1).astype(BF16)
    p = _dot(u, w_ref[...])
    q = p[:, :nq] * (AT_HEAD_DIM ** -0.5)
    k = p[:, nq:nq + nk]
    if rope:
        cos, sin = cos_ref[...], sin_ref[...]
        q = q * jnp.tile(cos, (1, nq // V7X_LANES)) + _swap_halves(q) * jnp.tile(sin, (1, nq // V7X_LANES))
        k = k * jnp.tile(cos, (1, nk // V7X_LANES)) + _swap_halves(k) * jnp.tile(sin, (1, nk // V7X_LANES))
    q_ref[...] = q.astype(BF16)
    k_ref[...] = k.astype(BF16)
    v_ref[...] = p[:, nq + nk:].astype(BF16)


def _qkv(h, st, mod, w, tables):
    rows, d = h.shape
    tm = ROW_TILE
    nq = AT_HEADS * AT_HEAD_DIM
    nk = AT_KV_HEADS * 2 * AT_HEAD_DIM
    rope = tables is not None
    in_specs = [_row_spec(tm, d), st.mod_spec(tm, d), _resident(w.shape)]
    args = [h, mod, w]
    if rope:
        tiles_per_seq = st.seq // tm
        tab_spec = pl.BlockSpec((tm, V7X_LANES), lambda i: (i % tiles_per_seq, 0))
        in_specs += [tab_spec, tab_spec]
        args += list(tables)
    return pl.pallas_call(
        functools.partial(_qkv_kernel, rope=rope),
        grid=(rows // tm,),
        in_specs=in_specs,
        out_specs=[_row_spec(tm, nq), _row_spec(tm, nk), _row_spec(tm, nk)],
        out_shape=[jax.ShapeDtypeStruct((rows, nq), BF16),
                   jax.ShapeDtypeStruct((rows, nk), BF16),
                   jax.ShapeDtypeStruct((rows, nk), BF16)],
        compiler_params=_cparams("parallel"),
        name="attn_qkv_rope",
    )(*args)


def _attn_kernel(*refs, window, tq, seq, alpha):
    if window:
        (sink_ref, q_ref, kp_ref, km_ref, kn_ref, kc_ref, vp_ref, vm_ref, vn_ref, vc_ref,
         h_ref, mod_ref, wo_ref, lng_ref, lnb_ref, o_ref) = refs
        k_parts = (kp_ref, km_ref, kn_ref, kc_ref)
        v_parts = (vp_ref, vm_ref, vn_ref, vc_ref)
    else:
        sink_ref, q_ref, kc_ref, vc_ref, h_ref, mod_ref, wo_ref, lng_ref, lnb_ref, o_ref = refs
        k_parts = (kc_ref,)
        v_parts = (vc_ref,)
    n_ctx = kc_ref.shape[0]
    lane = lax.broadcasted_iota(jnp.int32, (1, V7X_LANES), 1)
    lo = lane < AT_HEAD_DIM
    first_head = lax.broadcasted_iota(jnp.int32, (2 * tq, 1), 0) < tq

    if window:
        start = (pl.program_id(0) % (seq // tq)) * tq
        n_win = tq + 2 * AT_WINDOW
        q_pos = start + lax.broadcasted_iota(jnp.int32, (2 * tq, 1), 0) % tq
        k_idx = lax.broadcasted_iota(jnp.int32, (1, n_win + n_ctx), 1)
        k_pos = start - AT_WINDOW + k_idx
        valid = ((jnp.abs(q_pos - k_pos) <= AT_WINDOW) & (k_pos >= 0) & (k_pos < seq)) | (k_idx >= n_win)

    pairs = []
    for p in range(AT_HEADS // 2):
        grp = (2 * p) // (AT_HEADS // AT_KV_HEADS)
        cols = slice(grp * V7X_LANES, (grp + 1) * V7X_LANES)
        qp = q_ref[:, p * V7X_LANES:(p + 1) * V7X_LANES]
        zero = jnp.zeros_like(qp)
        qs = jnp.concatenate([jnp.where(lo, qp, zero), jnp.where(lo, zero, qp)], axis=0)
        kc = jnp.concatenate([r[:, cols] for r in k_parts], axis=0)
        vc = jnp.concatenate([r[:, cols] for r in v_parts], axis=0)
        s = lax.dot_general(qs, kc, (((1,), (1,)), ((), ())), preferred_element_type=F32)
        if window:
            s = jnp.where(valid, s, -jnp.inf)
        sk = jnp.where(first_head, sink_ref[2 * p], sink_ref[2 * p + 1])
        mx = jnp.maximum(jnp.max(s, axis=-1, keepdims=True), sk)
        e = jnp.exp(s - mx)
        den = jnp.sum(e, axis=-1, keepdims=True) + jnp.exp(sk - mx)
        o2 = _dot(e.astype(BF16), vc) / den
        pairs.append(jnp.where(lo, o2[:tq], o2[tq:]))
    o = jnp.concatenate(pairs, axis=1).astype(BF16)
    y = _dot(o, wo_ref[...])
    h = h_ref[...]
    gate = mod_ref[pl.ds(5, 1), :]
    o_ref[...] = _post_norm(h, y, gate, 1.0, lng_ref[...], lnb_ref[...], alpha)


def _attn(h, st, mod, sink, q, k, v, kctx, vctx, wo, lng, lnb, alpha, n_ctx, window):
    rows, d = h.shape
    nq, nk = q.shape[1], kctx.shape[1]
    tq = ROW_TILE
    tiles_per_seq = st.seq // tq
    smem = pl.BlockSpec(memory_space=pltpu.SMEM)
    ctx_spec = pl.BlockSpec((n_ctx, nk), lambda i: (i // tiles_per_seq, 0))
    if window:
        w_blk = AT_WINDOW
        per_tile = tq // w_blk
        last = rows // w_blk - 1
        per_seq = st.seq // w_blk

        def prev_map(i):
            return (jnp.maximum(i * per_tile - 1, 0), 0)

        def next_map(i):
            return (jnp.minimum((i + 1) * per_tile, last), 0)

        kv_specs = [pl.BlockSpec((w_blk, nk), prev_map), _row_spec(tq, nk), pl.BlockSpec((w_blk, nk), next_map),
                    ctx_spec]
        in_specs = [smem, _row_spec(tq, nq)] + kv_specs + kv_specs
        args = [sink, q, k, k, k, kctx, v, v, v, vctx]
        del per_seq
    else:
        in_specs = [smem, _row_spec(tq, nq), ctx_spec, ctx_spec]
        args = [sink, q, kctx, vctx]
    in_specs += [_row_spec(tq, d), st.mod_spec(tq, d), _resident(wo.shape), _resident((1, d)), _resident((1, d))]
    args += [h, mod, wo, lng, lnb]
    return pl.pallas_call(
        functools.partial(_attn_kernel, window=window, tq=tq, seq=st.seq, alpha=alpha),
        grid=(rows // tq,),
        in_specs=in_specs,
        out_specs=_row_spec(tq, d),
        out_shape=jax.ShapeDtypeStruct((rows, d), F32),
        compiler_params=_cparams("parallel"),
        name="attn_window" if window else "attn_context",
    )(*args)


def _attn_weights(w_qkv):
    d = AT_HEAD_DIM
    deint = np.concatenate([np.arange(0, d, 2), np.arange(1, d, 2)])
    q_cols = np.concatenate([h * d + deint for h in range(AT_HEADS)])
    k0 = AT_HEADS * d
    k_cols = np.concatenate([np.tile(k0 + g * d + deint, 2) for g in range(AT_KV_HEADS)])
    v0 = k0 + AT_KV_HEADS * d
    v_cols = np.concatenate([np.tile(v0 + g * d + np.arange(d), 2) for g in range(AT_KV_HEADS)])
    return w_qkv[:, np.concatenate([q_cols, k_cols, v_cols])].astype(BF16)


def _ml_prep_kernel(hp_ref, h_ref, hn_ref, mod_ref, wxm_ref, wz_ref, cw_ref, cb_ref, bdqk_ref, bdv_ref,
                    wif_ref, bif_ref, q_ref, k_ref, v_ref, xc_ref, z_ref, g_ref, *, tm, seq, k_scale):
    e = xc_ref.shape[1]
    prev_ok, next_ok = _halo_valid(tm, seq)
    u_ext = _modulate(jnp.concatenate([hp_ref[...], h_ref[...], hn_ref[...]], axis=0), mod_ref, 1).astype(BF16)
    lo, hi = V7X_SUBLANES, V7X_SUBLANES + tm
    xm_ext = _dot(u_ext, wxm_ref[...])
    z_ref[...] = _dot(_modulate(h_ref[...], mod_ref, 1).astype(BF16), wz_ref[...])
    xm = xm_ext[lo:hi]
    xm_prev = xm_ext[lo - 1:lo, :] * prev_ok
    xm_next = xm_ext[hi:hi + 1, :] * next_ok
    xc = _silu(_conv3(xm_prev, xm, xm_next, cw_ref, tm) + cb_ref[...])
    xc_ref[...] = xc
    xcb = xc.astype(BF16)
    xmb = xm.astype(BF16)
    blk = V7X_MXU_DIM
    qs, ks, vs = [], [], []
    for j in range(e // blk):
        qk = _dot(xcb[:, j * blk:(j + 1) * blk], bdqk_ref[j])
        qs.append(qk[:, :blk])
        ks.append(qk[:, blk:])
        vs.append(_dot(xmb[:, j * blk:(j + 1) * blk], bdv_ref[j]))
    q = jnp.concatenate(qs, axis=1).astype(BF16)
    k = jnp.concatenate(ks, axis=1).astype(BF16)
    v = jnp.concatenate(vs, axis=1).astype(BF16)
    g_ref[...] = _dot(jnp.concatenate([q, k, v], axis=1), wif_ref[...]) + bif_ref[...]
    q_ref[...] = q
    k_ref[...] = (jnp.concatenate(ks, axis=1) * k_scale).astype(BF16)
    v_ref[...] = v


def _ml_prep(h, st, mod, wxm, wz, cw, cb, bdqk, bdv, wif, bif):
    rows, d = h.shape
    e = wxm.shape[1]
    tm = ROW_TILE
    prev, nxt = _halo_specs(tm, d, rows)
    k_scale = (e // ML_HEADS) ** -0.5
    big = lambda dt: jax.ShapeDtypeStruct((rows, e), dt)
    return pl.pallas_call(
        functools.partial(_ml_prep_kernel, tm=tm, seq=st.seq, k_scale=k_scale),
        grid=(rows // tm,),
        in_specs=[
            prev, _row_spec(tm, d), nxt, st.mod_spec(tm, d),
            _resident(wxm.shape), _resident(wz.shape), _resident(cw.shape), _resident(cb.shape),
            _resident(bdqk.shape), _resident(bdv.shape), _resident(wif.shape), _resident(bif.shape),
        ],
        out_specs=[_row_spec(tm, e)] * 5 + [_row_spec(tm, V7X_LANES)],
        out_shape=[big(BF16), big(BF16), big(BF16), big(F32), big(F32),
                   jax.ShapeDtypeStruct((rows, V7X_LANES), F32)],
        compiler_params=_cparams("parallel"),
        name="mlstm_prep",
    )(h, h, h, mod, wxm, wz, cw, cb, bdqk, bdv, wif, bif)


def _log_sigmoid(x):
    return jnp.minimum(x, 0.0) - jnp.log(1.0 + jnp.exp(-jnp.abs(x)))


def _split_dot_left(tri, x):
    hi = x.astype(BF16)
    lo = (x - hi.astype(F32)).astype(BF16)
    return _dot(tri, hi) + _dot(tri, lo)


def _split_dot_right(x, tri):
    hi = x.astype(BF16)
    lo = (x - hi.astype(F32)).astype(BF16)
    return _dot(hi, tri) + _dot(lo, tri)


def _ml_scan_kernel(*refs, ctx_out, t):
    (qc_ref, kc_ref, vc_ref, grc_ref, gcc_ref, ql_ref, kl_ref, vl_ref, grl_ref, gcl_ref) = refs[:10]
    if ctx_out:
        hl_ref, hc_ref, c_ref, n_ref, m_ref = refs[10:]
    else:
        hl_ref, c_ref, n_ref, m_ref = refs[10:]
        hc_ref = None
    direction = pl.program_id(2)
    step = pl.program_id(3)
    backward = direction == 1

    @pl.when(step == 0)
    def _():
        c_ref[...] = jnp.zeros_like(c_ref)
        n_ref[...] = jnp.zeros_like(n_ref)
        m_ref[...] = jnp.zeros_like(m_ref)

    def chunk(q_ref, k_ref, v_ref, gr_ref, gc_ref, out_ref):
        q, k, v = q_ref[...], k_ref[...], v_ref[...]
        ig_row, fg_row = gr_ref[0:1, :], gr_ref[1:2, :]
        ig_col, fg_col = gc_ref[:, 0:1], gc_ref[:, 1:2]
        ri = lax.broadcasted_iota(jnp.int32, (t, t), 0)
        ci = lax.broadcasted_iota(jnp.int32, (t, t), 1)
        sign = jnp.where(backward, -1, 1)
        before = sign * ci <= sign * ri
        before_t = sign * ri <= sign * ci
        lf_col = jnp.broadcast_to(_log_sigmoid(fg_col), (t, V7X_LANES))
        lf_row = jnp.broadcast_to(_log_sigmoid(fg_row), (V7X_SUBLANES, t))
        b_col = _split_dot_left(before.astype(BF16), lf_col)[:, 0:1]
        b_row = _split_dot_right(lf_row, before_t.astype(BF16))[0:1, :]
        g = jnp.min(b_row, axis=-1, keepdims=True)
        m_prev = m_ref[0:1, 0:1]

        if out_ref is not None:
            d_intra = jnp.where(before, b_col - b_row + ig_row, -jnp.inf)
            d_inter = b_col + m_prev
            m_t = jnp.maximum(d_inter, jnp.max(d_intra, axis=-1, keepdims=True))
            s = lax.dot_general(q, k, (((1,), (1,)), ((), ())), preferred_element_type=F32)
            w_intra = s * jnp.exp(d_intra - m_t)
            w_inter = jnp.exp(d_inter - m_t)
            num = _dot(w_intra.astype(BF16), v) + w_inter * _dot(q, c_ref[...].astype(BF16))
            qn = jnp.sum(q.astype(F32) * n_ref[0:1, :], axis=-1, keepdims=True)
            den = jnp.sum(w_intra, axis=-1, keepdims=True) + w_inter * qn
            out_ref[...] = num * (1.0 / jnp.maximum(jnp.abs(den), jnp.exp(-m_t)))

        a_col = g - b_col + ig_col
        m_new = jnp.maximum(g + m_prev, jnp.max(a_col, axis=0, keepdims=True))
        w_s = jnp.exp(a_col - m_new)
        decay = jnp.exp(g + m_prev - m_new)
        kw = k.astype(F32) * w_s
        c_ref[...] = decay * c_ref[...] + lax.dot_general(
            kw.astype(BF16), v, (((0,), (0,)), ((), ())), preferred_element_type=F32)
        n_new = decay * n_ref[0:1, :] + jnp.sum(kw, axis=0, keepdims=True)
        n_ref[...] = jnp.broadcast_to(n_new, n_ref.shape)
        m_ref[...] = jnp.broadcast_to(m_new, m_ref.shape)

    @pl.when(step == 0)
    def _():
        chunk(qc_ref, kc_ref, vc_ref, grc_ref, gcc_ref, hc_ref)

    @pl.when(step > 0)
    def _():
        chunk(ql_ref, kl_ref, vl_ref, grl_ref, gcl_ref, hl_ref)


def _ml_scan(qkv_c, gates_c, qkv_l, gates_l, batch, n_ctx, seq, ctx_out):
    qc, kc, vc = qkv_c
    ql, kl, vl = qkv_l
    grc, gcc = gates_c
    grl, gcl = gates_l
    e = ql.shape[1]
    dh = e // ML_HEADS
    t = ML_CHUNK
    assert n_ctx == t, "context length must equal the mLSTM chunk length"
    ncl = seq // t

    def lat_chunk(dr, s):
        c = jnp.maximum(s - 1, 0)
        return jnp.where(dr == 1, ncl - 1 - c, c)

    ctx_rows = pl.BlockSpec((t, dh), lambda b, h, dr, s: (b, h))
    lat_rows = pl.BlockSpec((t, dh), lambda b, h, dr, s: (b * ncl + lat_chunk(dr, s), h))
    ctx_grow = pl.BlockSpec((None, None, None, 2, t), lambda b, h, dr, s: (b, dr, h, 0, 0))
    ctx_gcol = pl.BlockSpec((None, None, None, t, 2), lambda b, h, dr, s: (b, dr, h, 0, 0))
    lat_grow = pl.BlockSpec((None, None, None, 2, t), lambda b, h, dr, s: (b, dr, h, 0, lat_chunk(dr, s)))
    lat_gcol = pl.BlockSpec((None, None, None, t, 2), lambda b, h, dr, s: (b, dr, h, lat_chunk(dr, s), 0))
    out_l = pl.BlockSpec((None, t, dh), lambda b, h, dr, s: (dr, b * ncl + lat_chunk(dr, s), h))
    out_specs = [out_l]
    out_shape = [jax.ShapeDtypeStruct((2, batch * seq, e), F32)]
    if ctx_out:
        out_specs.append(pl.BlockSpec((None, t, dh), lambda b, h, dr, s: (dr, b, h)))
        out_shape.append(jax.ShapeDtypeStruct((2, batch * n_ctx, e), F32))
    return pl.pallas_call(
        functools.partial(_ml_scan_kernel, ctx_out=ctx_out, t=t),
        grid=(batch, ML_HEADS, 2, ncl + 1),
        in_specs=[ctx_rows, ctx_rows, ctx_rows, ctx_grow, ctx_gcol,
                  lat_rows, lat_rows, lat_rows, lat_grow, lat_gcol],
        out_specs=out_specs,
        out_shape=out_shape,
        scratch_shapes=[pltpu.VMEM((dh, dh), F32), pltpu.VMEM((V7X_SUBLANES, dh), F32),
                        pltpu.VMEM((V7X_SUBLANES, V7X_LANES), F32)],
        compiler_params=_cparams("parallel", "parallel", "parallel", "arbitrary"),
        name="mlstm_scan",
    )(qc, kc, vc, grc, gcc, ql, kl, vl, grl, gcl)


def _ml_finish_kernel(hf_ref, hb_ref, z_ref, xc_ref, h_ref, mod_ref, ng_ref, skip_ref, wd_ref, lng_ref, lnb_ref,
                      o_ref, *, alpha):
    e = z_ref.shape[1]
    dh = e // ML_HEADS
    hh = (hf_ref[...] + hb_ref[...]) * _sigmoid(z_ref[...])
    parts = []
    for hd in range(ML_HEADS):
        x = hh[:, hd * dh:(hd + 1) * dh]
        mu = jnp.mean(x, axis=-1, keepdims=True)
        xz = x - mu
        var = jnp.mean(xz * xz, axis=-1, keepdims=True)
        parts.append(xz * lax.rsqrt(var + LN_EPS))
    hn = jnp.concatenate(parts, axis=1) * ng_ref[...]
    y = _dot((hn + skip_ref[...] * xc_ref[...]).astype(BF16), wd_ref[...])
    gate = mod_ref[pl.ds(5, 1), :]
    o_ref[...] = _post_norm(h_ref[...], y, gate, 1.0, lng_ref[...], lnb_ref[...], alpha)


def _ml_finish(hs, z, xc, h, st, mod, ng, skip, wd, lng, lnb, alpha):
    rows, d = h.shape
    e = z.shape[1]
    tm = ROW_TILE
    return pl.pallas_call(
        functools.partial(_ml_finish_kernel, alpha=alpha),
        grid=(rows // tm,),
        in_specs=[
            pl.BlockSpec((None, tm, e), lambda i: (0, i, 0)),
            pl.BlockSpec((None, tm, e), lambda i: (1, i, 0)),
            _row_spec(tm, e), _row_spec(tm, e), _row_spec(tm, d), st.mod_spec(tm, d),
            _resident((1, e)), _resident((1, e)), _resident(wd.shape), _resident((1, d)), _resident((1, d)),
        ],
        out_specs=_row_spec(tm, d),
        out_shape=jax.ShapeDtypeStruct((rows, d), F32),
        compiler_params=_cparams("parallel"),
        name="mlstm_finish",
    )(hs, hs, z, xc, h, mod, ng, skip, wd, lng, lnb)


def _blockdiag_tiles(w):
    nblk, bs, _ = w.shape
    per = V7X_MXU_DIM // bs
    eye = jnp.eye(per, dtype=w.dtype)
    tiles = w.reshape(nblk // per, per, bs, bs)
    dense = tiles[:, :, :, None, :] * eye[None, :, None, :, None]
    return dense.reshape(nblk // per, per * bs, per * bs)


def _gate_layouts(g, batch, length):
    h = ML_HEADS
    g = g[:, :4 * h].reshape(batch, length, 2, 2, h)
    return g.transpose(0, 2, 4, 3, 1), g.transpose(0, 2, 4, 1, 3)


def kernel(x, c, ctx, c_ctx, mod_w, mod_b, ln_g, ln_b, ffn_w_in, ffn_w_out, ml_w_up, ml_conv_w, ml_conv_b,
           ml_w_qkv, ml_w_if, ml_b_if, ml_skip, ml_norm_g, ml_w_down, at_w_qkv, at_sink, at_w_o, sc_w_in,
           sc_conv_w, sc_w_out):
    batch, seq, d = x.shape
    n_ctx = ctx.shape[1]
    depth = mod_w.shape[0]
    alpha = (2 * depth) ** 0.25
    f = ffn_w_out.shape[2]
    assert batch + 1 <= MOD_ROWS and seq % ROW_TILE == 0 and n_ctx % ROW_TILE == 0

    lat = _Stream(batch * seq, seq, 0, seq)
    con = _Stream(batch * n_ctx, n_ctx, batch, batch * n_ctx)
    hl = x.reshape(batch * seq, d)
    hc = ctx.reshape(batch * n_ctx, d)

    cond = jnp.zeros((MOD_ROWS, d), F32).at[:batch].set(c).at[batch].set(c_ctx)
    mods = _mod_all(cond, mod_w, mod_b).reshape(depth, MOD_ROWS, N_MOD, d)

    ffn_tm = 512
    for i in range(depth):
        kind, j, last = i % N_MIXERS, i // N_MIXERS, i == depth - 1
        mod = mods[i]
        ln = lambda s: (ln_g[i, s][None], ln_b[i, s][None])

        def ffn_both(hl, hc, s, do_ctx):
            w_in, w_out = ffn_w_in[i, s], ffn_w_out[i, s]
            wg, wv, wo = w_in[:, :f].astype(BF16), w_in[:, f:].astype(BF16), w_out.astype(BF16)
            slot = 2 * s
            hl = _ffn(hl, lat, mod, wg, wv, wo, *ln(slot), slot, alpha, ffn_tm)
            if do_ctx:
                hc = _ffn(hc, con, mod, wg, wv, wo, *ln(slot), slot, alpha, ffn_tm)
            return hl, hc

        hl, hc = ffn_both(hl, hc, 0, True)

        if kind == 0:
            e = ml_w_up.shape[2] // 2
            wxm, wz = ml_w_up[j][:, :e].astype(BF16), ml_w_up[j][:, e:].astype(BF16)
            bdq, bdk, bdv = (_blockdiag_tiles(ml_w_qkv[j, a]).astype(BF16) for a in range(3))
            bdqk = jnp.concatenate([bdq, bdk], axis=2)
            n_g = 4 * ML_HEADS
            wif = jnp.concatenate([ml_w_if[j, 0], ml_w_if[j, 1]], axis=1)
            wif = jnp.pad(wif, ((0, 0), (0, V7X_LANES - n_g))).astype(BF16)
            bif = jnp.pad(jnp.concatenate([ml_b_if[j, 0], ml_b_if[j, 1]]), (0, V7X_LANES - n_g))[None]
            prep = lambda h, st: _ml_prep(h, st, mod, wxm, wz, ml_conv_w[j], ml_conv_b[j][None], bdqk, bdv, wif, bif)
            qc, kc, vc, xcc, zc, gc = prep(hc, con)
            ql, kl, vl, xcl, zl, gl = prep(hl, lat)
            outs = _ml_scan((qc, kc, vc), _gate_layouts(gc, batch, n_ctx), (ql, kl, vl),
                            _gate_layouts(gl, batch, seq), batch, n_ctx, seq, not last)
            fin = lambda hs, z, xc, h, st: _ml_finish(
                hs, z, xc, h, st, mod, ml_norm_g[j][None], ml_skip[j][None], ml_w_down[j].astype(BF16),
                *ln(1), alpha)
            hl = fin(outs[0], zl, xcl, hl, lat)
            if not last:
                hc = fin(outs[1], zc, xcc, hc, con)
        elif kind == 1:
            w = _attn_weights(at_w_qkv[j])
            wo = at_w_o[j].astype(BF16)
            ql, kl, vl = _qkv(hl, lat, mod, w, _rope_tables(seq))
            qc, kc, vc = _qkv(hc, con, mod, w, None)
            hl = _attn(hl, lat, mod, at_sink[j], ql, kl, vl, kc, vc, wo, *ln(1), alpha, n_ctx, True)
            if not last:
                hc = _attn(hc, con, mod, at_sink[j], qc, None, None, kc, vc, wo, *ln(1), alpha, n_ctx, False)
        else:
            w = sc_w_in[j]
            wb, wcx, wo = w[:, :d].astype(BF16), w[:, d:].astype(BF16), sc_w_out[j].astype(BF16)
            hl = _sconv(hl, lat, mod, wb, wcx, sc_conv_w[j], wo, *ln(1), alpha)
            if not last:
                hc = _sconv(hc, con, mod, wb, wcx, sc_conv_w[j], wo, *ln(1), alpha)

        hl, hc = ffn_both(hl, hc, 1, not last)
    return hl.reshape(batch, seq, d)
```

```python
import functools

import jax
import jax.numpy as jnp
import numpy as np
from jax import lax
from jax.experimental import pallas as pl
from jax.experimental.pallas import tpu as pltpu

N_MIXERS = 3
N_MOD = 9
LN_EPS = 1e-5
GRID_W = 64
ML_HEADS = 4
ML_QKV_BLOCK = 4
AT_HEADS = 16
AT_KV_HEADS = 4
AT_HEAD_DIM = 64
AT_WINDOW = 128
ROPE_BASE = 10000.0

V7X_LANES = 128
V7X_SUBLANES = 8
V7X_MXU_DIM = 256
V7X_VMEM_LIMIT_BYTES = 56 * 1024 * 1024

ROW_TILE = 256
ML_CHUNK = 256
MOD_ROWS = 16
MOD_COL_TILE = 1024

BF16 = jnp.bfloat16
F32 = jnp.float32


def _cparams(*sem):
    return pltpu.CompilerParams(dimension_semantics=sem, vmem_limit_bytes=V7X_VMEM_LIMIT_BYTES)


def _resident(shape):
    nd = len(shape)
    return pl.BlockSpec(shape, lambda *_: (0,) * nd, pipeline_mode=pl.Buffered(1))


def _dot(a, b):
    return jnp.dot(a, b, preferred_element_type=F32)


def _sigmoid(x):
    return 1.0 / (1.0 + jnp.exp(-x))


def _silu(x):
    return x * _sigmoid(x)


def _modulate(h, mod_ref, slot):
    shift = mod_ref[pl.ds(3 * slot, 1), :]
    scale = mod_ref[pl.ds(3 * slot + 1, 1), :]
    return h * (1.0 + scale) + shift


def _post_norm(h, y, gate, weight, g, b, alpha):
    z = alpha * h + (weight * gate) * y
    mu = jnp.mean(z, axis=-1, keepdims=True)
    zc = z - mu
    var = jnp.mean(zc * zc, axis=-1, keepdims=True)
    return zc * lax.rsqrt(var + LN_EPS) * g + b


def _mod_kernel(cond_ref, w_ref, b_ref, o_ref):
    cond = _silu(cond_ref[...]).astype(BF16)
    o_ref[...] = _dot(cond, w_ref[...].astype(BF16)) + b_ref[...]


def _mod_all(cond, mod_w, mod_b):
    depth, d, n = mod_w.shape
    tn = MOD_COL_TILE
    return pl.pallas_call(
        _mod_kernel,
        grid=(depth, n // tn),
        in_specs=[
            pl.BlockSpec((MOD_ROWS, d), lambda i, j: (0, 0)),
            pl.BlockSpec((None, d, tn), lambda i, j: (i, 0, j)),
            pl.BlockSpec((None, 1, tn), lambda i, j: (i, 0, j)),
        ],
        out_specs=pl.BlockSpec((None, MOD_ROWS, tn), lambda i, j: (i, 0, j)),
        out_shape=jax.ShapeDtypeStruct((depth, MOD_ROWS, n), F32),
        compiler_params=_cparams("parallel", "parallel"),
        name="mod_vectors",
    )(cond, mod_w, mod_b.reshape(depth, 1, n))


class _Stream:
    def __init__(self, rows, seq, mod_base, rows_per_mod):
        self.rows = rows
        self.seq = seq
        self.mod_base = mod_base
        self.rows_per_mod = rows_per_mod

    def mod_spec(self, tm, d):
        base, rpm = self.mod_base, self.rows_per_mod
        return pl.BlockSpec((None, N_MOD, d), lambda i: (base + (i * tm) // rpm, 0, 0))


def _row_spec(tm, width):
    return pl.BlockSpec((tm, width), lambda i: (i, 0))


def _ffn_kernel(h_ref, mod_ref, wg_ref, wv_ref, wo_ref, lng_ref, lnb_ref, o_ref, *, slot, alpha):
    h = h_ref[...]
    u = _modulate(h, mod_ref, slot).astype(BF16)
    g = _dot(u, wg_ref[...])
    v = _dot(u, wv_ref[...])
    a = (_silu(g) * v).astype(BF16)
    y = _dot(a, wo_ref[...])
    gate = mod_ref[pl.ds(3 * slot + 2, 1), :]
    o_ref[...] = _post_norm(h, y, gate, 0.5, lng_ref[...], lnb_ref[...], alpha)


def _ffn(h, st, mod, w_in, w_out, layer, half, lng, lnb, alpha, tm):
    rows, d = h.shape
    f = w_out.shape[2]
    slot = 2 * half

    def resident4(shape, col):
        return pl.BlockSpec((None, None) + shape, lambda i: (layer, half, 0, col), pipeline_mode=pl.Buffered(1))

    return pl.pallas_call(
        functools.partial(_ffn_kernel, slot=slot, alpha=alpha),
        grid=(rows // tm,),
        in_specs=[
            _row_spec(tm, d),
            st.mod_spec(tm, d),
            resident4((d, f), 0),
            resident4((d, f), 1),
            resident4((f, d), 0),
            _resident((1, d)),
            _resident((1, d)),
        ],
        out_specs=_row_spec(tm, d),
        out_shape=jax.ShapeDtypeStruct((rows, d), F32),
        compiler_params=_cparams("parallel"),
        name="ffn_swiglu",
    )(h, mod, w_in, w_in, w_out, lng, lnb)


def _halo_specs(tm, width, rows):
    r8 = tm // V7X_SUBLANES
    last8 = rows // V7X_SUBLANES - 1
    prev = pl.BlockSpec((V7X_SUBLANES, width), lambda i: (jnp.maximum(i * r8 - 1, 0), 0))
    nxt = pl.BlockSpec((V7X_SUBLANES, width), lambda i: (jnp.minimum((i + 1) * r8, last8), 0))
    return prev, nxt


def _halo_valid(tm, seq):
    tiles_per_seq = seq // tm
    pos = pl.program_id(0) % tiles_per_seq
    prev_ok = jnp.where(pos == 0, 0.0, 1.0).astype(F32)
    next_ok = jnp.where(pos == tiles_per_seq - 1, 0.0, 1.0).astype(F32)
    return prev_ok, next_ok


def _conv3(t_prev_row, t_main, t_next_row, w_ref, tm):
    row = lax.broadcasted_iota(jnp.int32, (tm, 1), 0)
    up = jnp.where(row == 0, t_prev_row, pltpu.roll(t_main, 1, axis=0))
    down = jnp.where(row == tm - 1, t_next_row, pltpu.roll(t_main, tm - 1, axis=0))
    return w_ref[0:1, :] * up + w_ref[1:2, :] * t_main + w_ref[2:3, :] * down


def _sconv_kernel(hp_ref, h_ref, hn_ref, mod_ref, wb_ref, wcx_ref, cw_ref, wo_ref, lng_ref, lnb_ref,
                  o_ref, *, alpha, tm, seq):
    d = h_ref.shape[1]
    h = h_ref[...]
    prev_ok, next_ok = _halo_valid(tm, seq)
    u_ext = _modulate(jnp.concatenate([hp_ref[...], h, hn_ref[...]], axis=0), mod_ref, 1).astype(BF16)
    lo, hi = V7X_SUBLANES, V7X_SUBLANES + tm
    bg = _dot(_modulate(h, mod_ref, 1).astype(BF16), wb_ref[...])
    cx = _dot(u_ext, wcx_ref[...])
    t_ext = cx[:, :d] * cx[:, d:]
    t_prev = t_ext[lo - 1:lo, :] * prev_ok
    t_next = t_ext[hi:hi + 1, :] * next_ok
    conv = _conv3(t_prev, t_ext[lo:hi], t_next, cw_ref, tm)
    y = _dot((bg * conv).astype(BF16), wo_ref[...])
    gate = mod_ref[pl.ds(5, 1), :]
    o_ref[...] = _post_norm(h, y, gate, 1.0, lng_ref[...], lnb_ref[...], alpha)


def _sconv(h, st, mod, wb, wcx, cw, wo, lng, lnb, alpha):
    rows, d = h.shape
    tm = ROW_TILE
    prev, nxt = _halo_specs(tm, d, rows)
    return pl.pallas_call(
        functools.partial(_sconv_kernel, alpha=alpha, tm=tm, seq=st.seq),
        grid=(rows // tm,),
        in_specs=[
            prev, _row_spec(tm, d), nxt,
            st.mod_spec(tm, d),
            _resident(wb.shape), _resident(wcx.shape), _resident(cw.shape), _resident(wo.shape),
            _resident((1, d)), _resident((1, d)),
        ],
        out_specs=_row_spec(tm, d),
        out_shape=jax.ShapeDtypeStruct((rows, d), F32),
        compiler_params=_cparams("parallel"),
        name="short_conv_mixer",
    )(h, h, h, mod, wb, wcx, cw, wo, lng, lnb)


def _rope_tables(n_tokens):
    rows_n = n_tokens // GRID_W
    rows = jnp.repeat(jnp.arange(rows_n), GRID_W).astype(F32)
    cols = jnp.tile(jnp.arange(GRID_W), rows_n).astype(F32)
    axis_dim = AT_HEAD_DIM // 2
    freqs = ROPE_BASE ** (-jnp.arange(0, axis_dim, 2, dtype=F32) / axis_dim)
    ang = jnp.concatenate([rows[:, None] * freqs, cols[:, None] * freqs], axis=-1)
    cos, sin = jnp.cos(ang), jnp.sin(ang)
    cos_t = jnp.tile(jnp.concatenate([cos, cos], axis=-1), (1, 2))
    sin_t = jnp.tile(jnp.concatenate([-sin, sin], axis=-1), (1, 2))
    return cos_t, sin_t


def _swap_halves(x):
    width = x.shape[1]
    half = AT_HEAD_DIM // 2
    lane = lax.broadcasted_iota(jnp.int32, (1, width), 1)
    lo = (lane % AT_HEAD_DIM) < half
    return jnp.where(lo, pltpu.roll(x, width - half, axis=1), pltpu.roll(x, half, axis=1))


def _qkv_kernel(*refs, rope):
    if rope:
        h_ref, mod_ref, w_ref, cos_ref, sin_ref, q_ref, k_ref, v_ref = refs
    else:
        h_ref, mod_ref, w_ref, q_ref, k_ref, v_ref = refs
    nq, nk = q_ref.shape[1], k_ref.shape[1]
    u = _modulate(h_ref[...], mod_ref, 1).astype(BF16)
    p = _dot(u, w_ref[...])
    q = p[:, :nq] * (AT_HEAD_DIM ** -0.5)
    k = p[:, nq:nq + nk]
    if rope:
        cos, sin = cos_ref[...], sin_ref[...]
        q = q * jnp.tile(cos, (1, nq // V7X_LANES)) + _swap_halves(q) * jnp.tile(sin, (1, nq // V7X_LANES))
        k = k * jnp.tile(cos, (1, nk // V7X_LANES)) + _swap_halves(k) * jnp.tile(sin, (1, nk // V7X_LANES))
    q_ref[...] = q.astype(BF16)
    k_ref[...] = k.astype(BF16)
    v_ref[...] = p[:, nq + nk:].astype(BF16)


def _qkv(h, st, mod, w, tables):
    rows, d = h.shape
    tm = ROW_TILE
    nq = AT_HEADS * AT_HEAD_DIM
    nk = AT_KV_HEADS * 2 * AT_HEAD_DIM
    rope = tables is not None
    in_specs = [_row_spec(tm, d), st.mod_spec(tm, d), _resident(w.shape)]
    args = [h, mod, w]
    if rope:
        tiles_per_seq = st.seq // tm
        tab_spec = pl.BlockSpec((tm, V7X_LANES), lambda i: (i % tiles_per_seq, 0))
        in_specs += [tab_spec, tab_spec]
        args += list(tables)
    return pl.pallas_call(
        functools.partial(_qkv_kernel, rope=rope),
        grid=(rows // tm,),
        in_specs=in_specs,
        out_specs=[_row_spec(tm, nq), _row_spec(tm, nk), _row_spec(tm, nk)],
        out_shape=[jax.ShapeDtypeStruct((rows, nq), BF16),
                   jax.ShapeDtypeStruct((rows, nk), BF16),
                   jax.ShapeDtypeStruct((rows, nk), BF16)],
        compiler_params=_cparams("parallel"),
        name="attn_qkv_rope",
    )(*args)


def _attn_kernel(*refs, window, tq, seq, alpha):
    if window:
        (sink_ref, q_ref, kp_ref, km_ref, kn_ref, kc_ref, vp_ref, vm_ref, vn_ref, vc_ref,
         h_ref, mod_ref, wo_ref, lng_ref, lnb_ref, o_ref) = refs
        k_parts = (kp_ref, km_ref, kn_ref, kc_ref)
        v_parts = (vp_ref, vm_ref, vn_ref, vc_ref)
    else:
        sink_ref, q_ref, kc_ref, vc_ref, h_ref, mod_ref, wo_ref, lng_ref, lnb_ref, o_ref = refs
        k_parts = (kc_ref,)
        v_parts = (vc_ref,)
    n_ctx = kc_ref.shape[0]
    lane = lax.broadcasted_iota(jnp.int32, (1, V7X_LANES), 1)
    lo = lane < AT_HEAD_DIM
    first_head = lax.broadcasted_iota(jnp.int32, (2 * tq, 1), 0) < tq

    if window:
        start = (pl.program_id(0) % (seq // tq)) * tq
        n_win = tq + 2 * AT_WINDOW
        q_pos = start + lax.broadcasted_iota(jnp.int32, (2 * tq, 1), 0) % tq
        k_idx = lax.broadcasted_iota(jnp.int32, (1, n_win + n_ctx), 1)
        k_pos = start - AT_WINDOW + k_idx
        valid = ((jnp.abs(q_pos - k_pos) <= AT_WINDOW) & (k_pos >= 0) & (k_pos < seq)) | (k_idx >= n_win)

    pairs = []
    for p in range(AT_HEADS // 2):
        grp = (2 * p) // (AT_HEADS // AT_KV_HEADS)
        cols = slice(grp * V7X_LANES, (grp + 1) * V7X_LANES)
        qp = q_ref[:, p * V7X_LANES:(p + 1) * V7X_LANES]
        zero = jnp.zeros_like(qp)
        qs = jnp.concatenate([jnp.where(lo, qp, zero), jnp.where(lo, zero, qp)], axis=0)
        kc = jnp.concatenate([r[:, cols] for r in k_parts], axis=0)
        vc = jnp.concatenate([r[:, cols] for r in v_parts], axis=0)
        s = lax.dot_general(qs, kc, (((1,), (1,)), ((), ())), preferred_element_type=F32)
        if window:
            s = jnp.where(valid, s, -jnp.inf)
        sk = jnp.where(first_head, sink_ref[2 * p], sink_ref[2 * p + 1])
        mx = jnp.maximum(jnp.max(s, axis=-1, keepdims=True), sk)
        e = jnp.exp(s - mx)
        den = jnp.sum(e, axis=-1, keepdims=True) + jnp.exp(sk - mx)
        o2 = _dot(e.astype(BF16), vc) / den
        pairs.append(jnp.where(lo, o2[:tq], o2[tq:]))
    o = jnp.concatenate(pairs, axis=1).astype(BF16)
    y = _dot(o, wo_ref[...])
    h = h_ref[...]
    gate = mod_ref[pl.ds(5, 1), :]
    o_ref[...] = _post_norm(h, y, gate, 1.0, lng_ref[...], lnb_ref[...], alpha)


def _attn(h, st, mod, sink, q, k, v, kctx, vctx, wo, lng, lnb, alpha, n_ctx, window):
    rows, d = h.shape
    nq, nk = q.shape[1], kctx.shape[1]
    tq = ROW_TILE
    tiles_per_seq = st.seq // tq
    smem = pl.BlockSpec(memory_space=pltpu.SMEM)
    ctx_spec = pl.BlockSpec((n_ctx, nk), lambda i: (i // tiles_per_seq, 0))
    if window:
        w_blk = AT_WINDOW
        per_tile = tq // w_blk
        last = rows // w_blk - 1
        per_seq = st.seq // w_blk

        def prev_map(i):
            return (jnp.maximum(i * per_tile - 1, 0), 0)

        def next_map(i):
            return (jnp.minimum((i + 1) * per_tile, last), 0)

        kv_specs = [pl.BlockSpec((w_blk, nk), prev_map), _row_spec(tq, nk), pl.BlockSpec((w_blk, nk), next_map),
                    ctx_spec]
        in_specs = [smem, _row_spec(tq, nq)] + kv_specs + kv_specs
        args = [sink, q, k, k, k, kctx, v, v, v, vctx]
        del per_seq
    else:
        in_specs = [smem, _row_spec(tq, nq), ctx_spec, ctx_spec]
        args = [sink, q, kctx, vctx]
    in_specs += [_row_spec(tq, d), st.mod_spec(tq, d), _resident(wo.shape), _resident((1, d)), _resident((1, d))]
    args += [h, mod, wo, lng, lnb]
    return pl.pallas_call(
        functools.partial(_attn_kernel, window=window, tq=tq, seq=st.seq, alpha=alpha),
        grid=(rows // tq,),
        in_specs=in_specs,
        out_specs=_row_spec(tq, d),
        out_shape=jax.ShapeDtypeStruct((rows, d), F32),
        compiler_params=_cparams("parallel"),
        name="attn_window" if window else "attn_context",
    )(*args)


def _attn_weights(w_qkv):
    d = AT_HEAD_DIM
    deint = np.concatenate([np.arange(0, d, 2), np.arange(1, d, 2)])
    q_cols = np.concatenate([h * d + deint for h in range(AT_HEADS)])
    k0 = AT_HEADS * d
    k_cols = np.concatenate([np.tile(k0 + g * d + deint, 2) for g in range(AT_KV_HEADS)])
    v0 = k0 + AT_KV_HEADS * d
    v_cols = np.concatenate([np.tile(v0 + g * d + np.arange(d), 2) for g in range(AT_KV_HEADS)])
    return w_qkv[:, np.concatenate([q_cols, k_cols, v_cols])].astype(BF16)


def _ml_prep_kernel(hp_ref, h_ref, hn_ref, mod_ref, wxm_ref, wz_ref, cw_ref, cb_ref, bdqk_ref, bdv_ref,
                    wif_ref, bif_ref, q_ref, kt_ref, v_ref, xc_ref, z_ref, g_ref, *, tm, seq, k_scale):
    e = xc_ref.shape[1]
    prev_ok, next_ok = _halo_valid(tm, seq)
    u_ext = _modulate(jnp.concatenate([hp_ref[...], h_ref[...], hn_ref[...]], axis=0), mod_ref, 1).astype(BF16)
    lo, hi = V7X_SUBLANES, V7X_SUBLANES + tm
    xm_ext = _dot(u_ext, wxm_ref[...])
    z_ref[...] = _dot(_modulate(h_ref[...], mod_ref, 1).astype(BF16), wz_ref[...])
    xm = xm_ext[lo:hi]
    xm_prev = xm_ext[lo - 1:lo, :] * prev_ok
    xm_next = xm_ext[hi:hi + 1, :] * next_ok
    xc = _silu(_conv3(xm_prev, xm, xm_next, cw_ref, tm) + cb_ref[...])
    xc_ref[...] = xc
    xcb = xc.astype(BF16)
    xmb = xm.astype(BF16)
    blk = V7X_MXU_DIM
    qs, ks, vs = [], [], []
    for j in range(e // blk):
        qk = _dot(xcb[:, j * blk:(j + 1) * blk], bdqk_ref[j])
        qs.append(qk[:, :blk])
        ks.append(qk[:, blk:])
        vs.append(_dot(xmb[:, j * blk:(j + 1) * blk], bdv_ref[j]))
    q = jnp.concatenate(qs, axis=1).astype(BF16)
    k = jnp.concatenate(ks, axis=1)
    v = jnp.concatenate(vs, axis=1).astype(BF16)
    gates = _dot(jnp.concatenate([q, k.astype(BF16), v], axis=1), wif_ref[...]) + bif_ref[...]
    q_ref[...] = q
    kt_ref[...] = (k * k_scale).T.astype(BF16)
    v_ref[...] = v

    n_h = ML_HEADS
    lf = _log_sigmoid(gates)
    lf_hi = lf.astype(BF16)
    lf_lo = (lf - lf_hi.astype(F32)).astype(BF16)
    ri = lax.broadcasted_iota(jnp.int32, (tm, tm), 0)
    ci = lax.broadcasted_iota(jnp.int32, (tm, tm), 1)
    lower = (ci <= ri).astype(BF16)
    upper = (ci >= ri).astype(BF16)
    prefix = _dot(lower, lf_hi) + _dot(lower, lf_lo)
    suffix = _dot(upper, lf_hi) + _dot(upper, lf_lo)
    lane = lax.broadcasted_iota(jnp.int32, (1, V7X_LANES), 1)
    g_ref[...] = jnp.where((lane >= n_h) & (lane < 2 * n_h), prefix,
                           jnp.where((lane >= 3 * n_h) & (lane < 4 * n_h), suffix, gates))


def _ml_prep(h, st, mod, wxm, wz, cw, cb, bdqk, bdv, wif, bif):
    rows, d = h.shape
    e = wxm.shape[1]
    tm = ML_CHUNK
    prev, nxt = _halo_specs(tm, d, rows)
    k_scale = (e // ML_HEADS) ** -0.5
    big = lambda dt: jax.ShapeDtypeStruct((rows, e), dt)
    return pl.pallas_call(
        functools.partial(_ml_prep_kernel, tm=tm, seq=st.seq, k_scale=k_scale),
        grid=(rows // tm,),
        in_specs=[
            prev, _row_spec(tm, d), nxt, st.mod_spec(tm, d),
            _resident(wxm.shape), _resident(wz.shape), _resident(cw.shape), _resident(cb.shape),
            _resident(bdqk.shape), _resident(bdv.shape), _resident(wif.shape), _resident(bif.shape),
        ],
        out_specs=[_row_spec(tm, e), pl.BlockSpec((e, tm), lambda i: (0, i))] + [_row_spec(tm, e)] * 3
        + [_row_spec(tm, V7X_LANES)],
        out_shape=[big(BF16), jax.ShapeDtypeStruct((e, rows), BF16), big(BF16), big(F32), big(F32),
                   jax.ShapeDtypeStruct((rows, V7X_LANES), F32)],
        compiler_params=_cparams("parallel"),
        name="mlstm_prep",
    )(h, h, h, mod, wxm, wz, cw, cb, bdqk, bdv, wif, bif)


def _log_sigmoid(x):
    return jnp.minimum(x, 0.0) - jnp.log(1.0 + jnp.exp(-jnp.abs(x)))


def _ml_chunk(q_ref, kt_ref, v_ref, gr_ref, gc_ref, out_ref, c_ref, cb_ref, n_ref, m_ref, *, backward, t):
    q, kt, v = q_ref[...], kt_ref[...], v_ref[...]
    i_idx = 2 * ML_HEADS * int(backward) + pl.program_id(1)
    b_idx = i_idx + ML_HEADS
    ig_row, b_row = gr_ref[pl.ds(i_idx, 1), :], gr_ref[pl.ds(b_idx, 1), :]
    lane = lax.broadcasted_iota(jnp.int32, (1, V7X_LANES), 1)
    b_col = jnp.sum(jnp.where(lane == b_idx, gc_ref[...], 0.0), axis=-1, keepdims=True)
    m_prev = m_ref[0:1, 0:1]
    g = jnp.min(b_row, axis=-1, keepdims=True)

    if out_ref is not None:
        ri = lax.broadcasted_iota(jnp.int32, (t, t), 0)
        ci = lax.broadcasted_iota(jnp.int32, (t, t), 1)
        before = (ci >= ri) if backward else (ci <= ri)
        r = jnp.where(before, ig_row - b_row, -jnp.inf)
        mx = jnp.maximum(jnp.max(r, axis=-1, keepdims=True), m_prev)
        w = _dot(q, kt) * jnp.exp(r - mx)
        w_inter = jnp.exp(m_prev - mx)
        qn = jnp.sum(q.astype(F32) * n_ref[0:1, :], axis=-1, keepdims=True)
        den = jnp.sum(w, axis=-1, keepdims=True) + w_inter * qn
        scale = 1.0 / jnp.maximum(jnp.abs(den), jnp.exp(-(b_col + mx)))
        out_ref[...] = (_dot(w.astype(BF16), v) + w_inter * _dot(q, cb_ref[...])) * scale

    a_row = g - b_row + ig_row
    m_new = jnp.maximum(g + m_prev, jnp.max(a_row, axis=-1, keepdims=True))
    w_s = jnp.exp(a_row - m_new).astype(BF16)
    decay = jnp.exp(g + m_prev - m_new)
    c_new = decay * c_ref[...] + _dot(kt * w_s, v)
    c_ref[...] = c_new
    cb_ref[...] = c_new.astype(BF16)
    w_s8 = jnp.broadcast_to(w_s, (V7X_SUBLANES * 2, t))
    n_ref[...] = decay * n_ref[...] + lax.dot_general(
        w_s8, kt, (((1,), (1,)), ((), ())), preferred_element_type=F32)
    m_ref[...] = jnp.broadcast_to(m_new, m_ref.shape)


def _ml_scan_kernel(*refs, ctx_out, t):
    (qc_ref, ktc_ref, vc_ref, grc_ref, gcc_ref,
     qf_ref, ktf_ref, vf_ref, grf_ref, gcf_ref,
     qb_ref, ktb_ref, vb_ref, grb_ref, gcb_ref) = refs[:15]
    n_out = 4 if ctx_out else 2
    outs = refs[15:15 + n_out]
    scratch = refs[15 + n_out:]
    hf_ref, hb_ref = outs[0], outs[1]
    hcf_ref, hcb_ref = (outs[2], outs[3]) if ctx_out else (None, None)
    state = (scratch[0:4], scratch[4:8])
    step = pl.program_id(2)

    @pl.when(step == 0)
    def _():
        for refs_d in state:
            for ref in refs_d:
                ref[...] = jnp.zeros_like(ref)
        _ml_chunk(qc_ref, ktc_ref, vc_ref, grc_ref, gcc_ref, hcf_ref, *state[0], backward=False, t=t)
        _ml_chunk(qc_ref, ktc_ref, vc_ref, grc_ref, gcc_ref, hcb_ref, *state[1], backward=True, t=t)

    @pl.when(step > 0)
    def _():
        _ml_chunk(qf_ref, ktf_ref, vf_ref, grf_ref, gcf_ref, hf_ref, *state[0], backward=False, t=t)
        _ml_chunk(qb_ref, ktb_ref, vb_ref, grb_ref, gcb_ref, hb_ref, *state[1], backward=True, t=t)


def _ml_scan(qkv_c, gates_c, qkv_l, gates_l, batch, n_ctx, seq, ctx_out):
    qc, ktc, vc = qkv_c
    ql, ktl, vl = qkv_l
    grc, gcc = gates_c
    grl, gcl = gates_l
    e = ql.shape[1]
    dh = e // ML_HEADS
    t = ML_CHUNK
    assert n_ctx == t, "context length must equal the mLSTM chunk length"
    ncl = seq // t
    fwd = lambda s: jnp.maximum(s - 1, 0)
    bwd = lambda s: ncl - 1 - jnp.maximum(s - 1, 0)

    n_gates = grl.shape[0]

    def chunk_specs(chunk):
        rows = pl.BlockSpec((t, dh), lambda b, h, s: (b * ncl + chunk(s), h))
        cols = pl.BlockSpec((dh, t), lambda b, h, s: (h, b * ncl + chunk(s)))
        g_rows = pl.BlockSpec((n_gates, t), lambda b, h, s: (0, b * ncl + chunk(s)))
        g_cols = pl.BlockSpec((t, V7X_LANES), lambda b, h, s: (b * ncl + chunk(s), 0))
        return [rows, cols, rows, g_rows, g_cols]

    ctx_rows = pl.BlockSpec((t, dh), lambda b, h, s: (b, h))
    ctx_specs = [ctx_rows, pl.BlockSpec((dh, t), lambda b, h, s: (h, b)), ctx_rows,
                 pl.BlockSpec((n_gates, t), lambda b, h, s: (0, b)),
                 pl.BlockSpec((t, V7X_LANES), lambda b, h, s: (b, 0))]
    in_specs = ctx_specs + chunk_specs(fwd) + chunk_specs(bwd)
    out_specs = [pl.BlockSpec((t, dh), lambda b, h, s: (b * ncl + fwd(s), h)),
                 pl.BlockSpec((t, dh), lambda b, h, s: (b * ncl + bwd(s), h))]
    out_shape = [jax.ShapeDtypeStruct((batch * seq, e), F32)] * 2
    if ctx_out:
        out_specs += [ctx_rows, ctx_rows]
        out_shape += [jax.ShapeDtypeStruct((batch * n_ctx, e), F32)] * 2
    per_dir = [pltpu.VMEM((dh, dh), F32), pltpu.VMEM((dh, dh), BF16), pltpu.VMEM((2 * V7X_SUBLANES, dh), F32),
               pltpu.VMEM((V7X_SUBLANES, V7X_LANES), F32)]
    return pl.pallas_call(
        functools.partial(_ml_scan_kernel, ctx_out=ctx_out, t=t),
        grid=(batch, ML_HEADS, ncl + 1),
        in_specs=in_specs,
        out_specs=out_specs,
        out_shape=out_shape,
        scratch_shapes=per_dir + per_dir,
        compiler_params=_cparams("parallel", "parallel", "arbitrary"),
        name="mlstm_scan",
    )(qc, ktc, vc, grc, gcc, ql, ktl, vl, grl, gcl, ql, ktl, vl, grl, gcl)


def _ml_finish_kernel(hf_ref, hb_ref, z_ref, xc_ref, h_ref, mod_ref, ng_ref, skip_ref, wd_ref, lng_ref, lnb_ref,
                      o_ref, *, alpha):
    e = z_ref.shape[1]
    dh = e // ML_HEADS
    hh = (hf_ref[...] + hb_ref[...]) * _sigmoid(z_ref[...])
    parts = []
    for hd in range(ML_HEADS):
        x = hh[:, hd * dh:(hd + 1) * dh]
        mu = jnp.mean(x, axis=-1, keepdims=True)
        xz = x - mu
        var = jnp.mean(xz * xz, axis=-1, keepdims=True)
        parts.append(xz * lax.rsqrt(var + LN_EPS))
    hn = jnp.concatenate(parts, axis=1) * ng_ref[...]
    y = _dot((hn + skip_ref[...] * xc_ref[...]).astype(BF16), wd_ref[...])
    gate = mod_ref[pl.ds(5, 1), :]
    o_ref[...] = _post_norm(h_ref[...], y, gate, 1.0, lng_ref[...], lnb_ref[...], alpha)


def _ml_finish(hf, hb, z, xc, h, st, mod, ng, skip, wd, lng, lnb, alpha):
    rows, d = h.shape
    e = z.shape[1]
    tm = ROW_TILE
    return pl.pallas_call(
        functools.partial(_ml_finish_kernel, alpha=alpha),
        grid=(rows // tm,),
        in_specs=[
            _row_spec(tm, e), _row_spec(tm, e),
            _row_spec(tm, e), _row_spec(tm, e), _row_spec(tm, d), st.mod_spec(tm, d),
            _resident((1, e)), _resident((1, e)), _resident(wd.shape), _resident((1, d)), _resident((1, d)),
        ],
        out_specs=_row_spec(tm, d),
        out_shape=jax.ShapeDtypeStruct((rows, d), F32),
        compiler_params=_cparams("parallel"),
        name="mlstm_finish",
    )(hf, hb, z, xc, h, mod, ng, skip, wd, lng, lnb)


def _blockdiag_tiles(w):
    nblk, bs, _ = w.shape
    side = V7X_MXU_DIM
    rows = w.reshape(nblk * bs // side, side, bs)
    idx = np.arange(side) // bs
    on_diagonal = jnp.asarray(idx[:, None] == idx[None, :])
    return jnp.where(on_diagonal, jnp.tile(rows, (1, 1, side // bs)), 0.0)


def _gate_layouts(g):
    return g[:, :4 * ML_HEADS].T, g


def kernel(x, c, ctx, c_ctx, mod_w, mod_b, ln_g, ln_b, ffn_w_in, ffn_w_out, ml_w_up, ml_conv_w, ml_conv_b,
           ml_w_qkv, ml_w_if, ml_b_if, ml_skip, ml_norm_g, ml_w_down, at_w_qkv, at_sink, at_w_o, sc_w_in,
           sc_conv_w, sc_w_out):
    batch, seq, d = x.shape
    n_ctx = ctx.shape[1]
    depth = mod_w.shape[0]
    alpha = (2 * depth) ** 0.25
    f = ffn_w_out.shape[2]
    assert batch + 1 <= MOD_ROWS and seq % ROW_TILE == 0 and n_ctx % ROW_TILE == 0

    lat = _Stream(batch * seq, seq, 0, seq)
    con = _Stream(batch * n_ctx, n_ctx, batch, batch * n_ctx)
    hl = x.reshape(batch * seq, d)
    hc = ctx.reshape(batch * n_ctx, d)

    cond = jnp.zeros((MOD_ROWS, d), F32).at[:batch].set(c).at[batch].set(c_ctx)
    mods = _mod_all(cond, mod_w, mod_b).reshape(depth, MOD_ROWS, N_MOD, d)

    ffn_tm = 512
    ffn_in, ffn_out = ffn_w_in.astype(BF16), ffn_w_out.astype(BF16)
    for i in range(depth):
        kind, j, last = i % N_MIXERS, i // N_MIXERS, i == depth - 1
        mod = mods[i]
        ln = lambda s: (ln_g[i, s][None], ln_b[i, s][None])

        def ffn_both(hl, hc, s, do_ctx):
            hl = _ffn(hl, lat, mod, ffn_in, ffn_out, i, s, *ln(2 * s), alpha, ffn_tm)
            if do_ctx:
                hc = _ffn(hc, con, mod, ffn_in, ffn_out, i, s, *ln(2 * s), alpha, ffn_tm)
            return hl, hc

        hl, hc = ffn_both(hl, hc, 0, True)

        if kind == 0:
            e = ml_w_up.shape[2] // 2
            wxm, wz = ml_w_up[j][:, :e].astype(BF16), ml_w_up[j][:, e:].astype(BF16)
            bdq, bdk, bdv = (_blockdiag_tiles(ml_w_qkv[j, a]).astype(BF16) for a in range(3))
            bdqk = jnp.concatenate([bdq, bdk], axis=2)
            n_g = 4 * ML_HEADS
            wif = jnp.concatenate([ml_w_if[j, 0], ml_w_if[j, 1]], axis=1)
            wif = jnp.pad(wif, ((0, 0), (0, V7X_LANES - n_g))).astype(BF16)
            bif = jnp.pad(jnp.concatenate([ml_b_if[j, 0], ml_b_if[j, 1]]), (0, V7X_LANES - n_g))[None]
            prep = lambda h, st: _ml_prep(h, st, mod, wxm, wz, ml_conv_w[j], ml_conv_b[j][None], bdqk, bdv, wif, bif)
            qc, kc, vc, xcc, zc, gc = prep(hc, con)
            ql, kl, vl, xcl, zl, gl = prep(hl, lat)
            outs = _ml_scan((qc, kc, vc), _gate_layouts(gc), (ql, kl, vl), _gate_layouts(gl),
                            batch, n_ctx, seq, not last)
            fin = lambda hf, hb, z, xc, h, st: _ml_finish(
                hf, hb, z, xc, h, st, mod, ml_norm_g[j][None], ml_skip[j][None], ml_w_down[j].astype(BF16),
                *ln(1), alpha)
            hl = fin(outs[0], outs[1], zl, xcl, hl, lat)
            if not last:
                hc = fin(outs[2], outs[3], zc, xcc, hc, con)
        elif kind == 1:
            w = _attn_weights(at_w_qkv[j])
            wo = at_w_o[j].astype(BF16)
            ql, kl, vl = _qkv(hl, lat, mod, w, _rope_tables(seq))
            qc, kc, vc = _qkv(hc, con, mod, w, None)
            hl = _attn(hl, lat, mod, at_sink[j], ql, kl, vl, kc, vc, wo, *ln(1), alpha, n_ctx, True)
            if not last:
                hc = _attn(hc, con, mod, at_sink[j], qc, None, None, kc, vc, wo, *ln(1), alpha, n_ctx, False)
        else:
            w = sc_w_in[j]
            wb, wcx, wo = w[:, :d].astype(BF16), w[:, d:].astype(BF16), sc_w_out[j].astype(BF16)
            hl = _sconv(hl, lat, mod, wb, wcx, sc_conv_w[j], wo, *ln(1), alpha)
            if not last:
                hc = _sconv(hc, con, mod, wb, wcx, sc_conv_w[j], wo, *ln(1), alpha)

        hl, hc = ffn_both(hl, hc, 1, not last)
    return hl.reshape(batch, seq, d)
```

```python
import functools

import jax
import jax.numpy as jnp
import numpy as np
from jax import lax
from jax.experimental import pallas as pl
from jax.experimental.pallas import tpu as pltpu

N_MIXERS = 3
N_MOD = 9
LN_EPS = 1e-5
GRID_W = 64
ML_HEADS = 4
ML_QKV_BLOCK = 4
AT_HEADS = 16
AT_KV_HEADS = 4
AT_HEAD_DIM = 64
AT_WINDOW = 128
ROPE_BASE = 10000.0

V7X_LANES = 128
V7X_SUBLANES = 8
V7X_MXU_DIM = 256
V7X_VMEM_LIMIT_BYTES = 56 * 1024 * 1024

ROW_TILE = 256
WIDE_ROW_TILE = 512
ML_CHUNK = 256
ML_SCAN_HEADS = 2
MOD_ROWS = 16
MOD_COL_TILE = 1024

BF16 = jnp.bfloat16
F32 = jnp.float32


def _cparams(*sem):
    return pltpu.CompilerParams(dimension_semantics=sem, vmem_limit_bytes=V7X_VMEM_LIMIT_BYTES)


def _resident(shape):
    nd = len(shape)
    return pl.BlockSpec(shape, lambda *_: (0,) * nd, pipeline_mode=pl.Buffered(1))


def _dot(a, b):
    return jnp.dot(a, b, preferred_element_type=F32)


def _sigmoid(x):
    return 1.0 / (1.0 + jnp.exp(-x))


def _silu(x):
    return x * _sigmoid(x)


def _modulate(h, mod_ref, slot):
    shift = mod_ref[pl.ds(3 * slot, 1), :]
    scale = mod_ref[pl.ds(3 * slot + 1, 1), :]
    return h * (1.0 + scale) + shift


def _post_norm(h, y, gate, weight, g, b, alpha):
    z = alpha * h + (weight * gate) * y
    mu = jnp.mean(z, axis=-1, keepdims=True)
    zc = z - mu
    var = jnp.mean(zc * zc, axis=-1, keepdims=True)
    return zc * lax.rsqrt(var + LN_EPS) * g + b


def _mod_kernel(cond_ref, w_ref, b_ref, o_ref):
    cond = _silu(cond_ref[...]).astype(BF16)
    o_ref[...] = _dot(cond, w_ref[...].astype(BF16)) + b_ref[...]


def _mod_all(cond, mod_w, mod_b):
    depth, d, n = mod_w.shape
    tn = MOD_COL_TILE
    return pl.pallas_call(
        _mod_kernel,
        grid=(depth, n // tn),
        in_specs=[
            pl.BlockSpec((MOD_ROWS, d), lambda i, j: (0, 0)),
            pl.BlockSpec((None, d, tn), lambda i, j: (i, 0, j)),
            pl.BlockSpec((None, 1, tn), lambda i, j: (i, 0, j)),
        ],
        out_specs=pl.BlockSpec((None, MOD_ROWS, tn), lambda i, j: (i, 0, j)),
        out_shape=jax.ShapeDtypeStruct((depth, MOD_ROWS, n), F32),
        compiler_params=_cparams("parallel", "parallel"),
        name="mod_vectors",
    )(cond, mod_w, mod_b.reshape(depth, 1, n))


class _Stream:
    def __init__(self, rows, seq, mod_base, rows_per_mod):
        self.rows = rows
        self.seq = seq
        self.mod_base = mod_base
        self.rows_per_mod = rows_per_mod

    def mod_spec(self, tm, d):
        base, rpm = self.mod_base, self.rows_per_mod
        return pl.BlockSpec((None, N_MOD, d), lambda i: (base + (i * tm) // rpm, 0, 0))


def _row_spec(tm, width):
    return pl.BlockSpec((tm, width), lambda i: (i, 0))


def _ffn_kernel(h_ref, mod_ref, wg_ref, wv_ref, wo_ref, lng_ref, lnb_ref, o_ref, *, slot, alpha):
    h = h_ref[...]
    u = _modulate(h, mod_ref, slot).astype(BF16)
    g = _dot(u, wg_ref[...])
    v = _dot(u, wv_ref[...])
    a = (_silu(g) * v).astype(BF16)
    y = _dot(a, wo_ref[...])
    gate = mod_ref[pl.ds(3 * slot + 2, 1), :]
    o_ref[...] = _post_norm(h, y, gate, 0.5, lng_ref[...], lnb_ref[...], alpha)


def _ffn(h, st, mod, w_in, w_out, layer, half, lng, lnb, alpha, tm):
    rows, d = h.shape
    f = w_out.shape[2]
    slot = 2 * half

    def resident4(shape, col):
        return pl.BlockSpec((None, None) + shape, lambda i: (layer, half, 0, col), pipeline_mode=pl.Buffered(1))

    return pl.pallas_call(
        functools.partial(_ffn_kernel, slot=slot, alpha=alpha),
        grid=(rows // tm,),
        in_specs=[
            _row_spec(tm, d),
            st.mod_spec(tm, d),
            resident4((d, f), 0),
            resident4((d, f), 1),
            resident4((f, d), 0),
            _resident((1, d)),
            _resident((1, d)),
        ],
        out_specs=_row_spec(tm, d),
        out_shape=jax.ShapeDtypeStruct((rows, d), F32),
        compiler_params=_cparams("parallel"),
        name="ffn_swiglu",
    )(h, mod, w_in, w_in, w_out, lng, lnb)


def _halo_specs(tm, width, rows):
    r8 = tm // V7X_SUBLANES
    last8 = rows // V7X_SUBLANES - 1
    prev = pl.BlockSpec((V7X_SUBLANES, width), lambda i: (jnp.maximum(i * r8 - 1, 0), 0))
    nxt = pl.BlockSpec((V7X_SUBLANES, width), lambda i: (jnp.minimum((i + 1) * r8, last8), 0))
    return prev, nxt


def _halo_valid(tm, seq):
    tiles_per_seq = seq // tm
    pos = pl.program_id(0) % tiles_per_seq
    prev_ok = jnp.where(pos == 0, 0.0, 1.0).astype(F32)
    next_ok = jnp.where(pos == tiles_per_seq - 1, 0.0, 1.0).astype(F32)
    return prev_ok, next_ok


def _conv3(t_prev_row, t_main, t_next_row, w_ref, tm):
    row = lax.broadcasted_iota(jnp.int32, (tm, 1), 0)
    up = jnp.where(row == 0, t_prev_row, pltpu.roll(t_main, 1, axis=0))
    down = jnp.where(row == tm - 1, t_next_row, pltpu.roll(t_main, tm - 1, axis=0))
    return w_ref[0:1, :] * up + w_ref[1:2, :] * t_main + w_ref[2:3, :] * down


def _sconv_kernel(hp_ref, h_ref, hn_ref, mod_ref, wb_ref, wcx_ref, cw_ref, wo_ref, lng_ref, lnb_ref,
                  o_ref, *, alpha, tm, seq):
    d = h_ref.shape[1]
    h = h_ref[...]
    prev_ok, next_ok = _halo_valid(tm, seq)
    u_ext = _modulate(jnp.concatenate([hp_ref[...], h, hn_ref[...]], axis=0), mod_ref, 1).astype(BF16)
    lo, hi = V7X_SUBLANES, V7X_SUBLANES + tm
    bg = _dot(_modulate(h, mod_ref, 1).astype(BF16), wb_ref[...])
    cx = _dot(u_ext, wcx_ref[...])
    t_ext = cx[:, :d] * cx[:, d:]
    t_prev = t_ext[lo - 1:lo, :] * prev_ok
    t_next = t_ext[hi:hi + 1, :] * next_ok
    conv = _conv3(t_prev, t_ext[lo:hi], t_next, cw_ref, tm)
    y = _dot((bg * conv).astype(BF16), wo_ref[...])
    gate = mod_ref[pl.ds(5, 1), :]
    o_ref[...] = _post_norm(h, y, gate, 1.0, lng_ref[...], lnb_ref[...], alpha)


def _sconv(h, st, mod, wb, wcx, cw, wo, lng, lnb, alpha):
    rows, d = h.shape
    tm = min(WIDE_ROW_TILE, st.seq)
    prev, nxt = _halo_specs(tm, d, rows)
    return pl.pallas_call(
        functools.partial(_sconv_kernel, alpha=alpha, tm=tm, seq=st.seq),
        grid=(rows // tm,),
        in_specs=[
            prev, _row_spec(tm, d), nxt,
            st.mod_spec(tm, d),
            _resident(wb.shape), _resident(wcx.shape), _resident(cw.shape), _resident(wo.shape),
            _resident((1, d)), _resident((1, d)),
        ],
        out_specs=_row_spec(tm, d),
        out_shape=jax.ShapeDtypeStruct((rows, d), F32),
        compiler_params=_cparams("parallel"),
        name="short_conv_mixer",
    )(h, h, h, mod, wb, wcx, cw, wo, lng, lnb)


def _rope_tables(n_tokens):
    rows_n = n_tokens // GRID_W
    rows = jnp.repeat(jnp.arange(rows_n), GRID_W).astype(F32)
    cols = jnp.tile(jnp.arange(GRID_W), rows_n).astype(F32)
    axis_dim = AT_HEAD_DIM // 2
    freqs = ROPE_BASE ** (-jnp.arange(0, axis_dim, 2, dtype=F32) / axis_dim)
    ang = jnp.concatenate([rows[:, None] * freqs, cols[:, None] * freqs], axis=-1)
    cos, sin = jnp.cos(ang), jnp.sin(ang)
    cos_t = jnp.tile(jnp.concatenate([cos, cos], axis=-1), (1, 2))
    sin_t = jnp.tile(jnp.concatenate([-sin, sin], axis=-1), (1, 2))
    return cos_t, sin_t


def _swap_halves(x):
    width = x.shape[1]
    half = AT_HEAD_DIM // 2
    lane = lax.broadcasted_iota(jnp.int32, (1, width), 1)
    lo = (lane % AT_HEAD_DIM) < half
    return jnp.where(lo, pltpu.roll(x, width - half, axis=1), pltpu.roll(x, half, axis=1))


def _qkv_kernel(*refs, rope):
    if rope:
        h_ref, mod_ref, w_ref, cos_ref, sin_ref, q_ref, k_ref, v_ref = refs
    else:
        h_ref, mod_ref, w_ref, q_ref, k_ref, v_ref = refs
    nq, nk = q_ref.shape[1], k_ref.shape[1]
    u = _modulate(h_ref[...], mod_ref, 1).astype(BF16)
    p = _dot(u, w_ref[...])
    q = p[:, :nq] * (AT_HEAD_DIM ** -0.5)
    k = p[:, nq:nq + nk]
    if rope:
        cos, sin = cos_ref[...], sin_ref[...]
        q = q * jnp.tile(cos, (1, nq // V7X_LANES)) + _swap_halves(q) * jnp.tile(sin, (1, nq // V7X_LANES))
        k = k * jnp.tile(cos, (1, nk // V7X_LANES)) + _swap_halves(k) * jnp.tile(sin, (1, nk // V7X_LANES))
    q_ref[...] = q.astype(BF16)
    k_ref[...] = k.astype(BF16)
    v_ref[...] = p[:, nq + nk:].astype(BF16)


def _qkv(h, st, mod, w, tables):
    rows, d = h.shape
    tm = min(WIDE_ROW_TILE, st.seq)
    nq = AT_HEADS * AT_HEAD_DIM
    nk = AT_KV_HEADS * 2 * AT_HEAD_DIM
    rope = tables is not None
    in_specs = [_row_spec(tm, d), st.mod_spec(tm, d), _resident(w.shape)]
    args = [h, mod, w]
    if rope:
        tiles_per_seq = st.seq // tm
        tab_spec = pl.BlockSpec((tm, V7X_LANES), lambda i: (i % tiles_per_seq, 0))
        in_specs += [tab_spec, tab_spec]
        args += list(tables)
    return pl.pallas_call(
        functools.partial(_qkv_kernel, rope=rope),
        grid=(rows // tm,),
        in_specs=in_specs,
        out_specs=[_row_spec(tm, nq), _row_spec(tm, nk), _row_spec(tm, nk)],
        out_shape=[jax.ShapeDtypeStruct((rows, nq), BF16),
                   jax.ShapeDtypeStruct((rows, nk), BF16),
                   jax.ShapeDtypeStruct((rows, nk), BF16)],
        compiler_params=_cparams("parallel"),
        name="attn_qkv_rope",
    )(*args)


def _attn_kernel(*refs, window, tq, seq, alpha):
    if window:
        (sink_ref, q_ref, kp_ref, km_ref, kn_ref, kc_ref, vp_ref, vm_ref, vn_ref, vc_ref,
         h_ref, mod_ref, wo_ref, lng_ref, lnb_ref, o_ref) = refs
        k_parts = (kp_ref, km_ref, kn_ref, kc_ref)
        v_parts = (vp_ref, vm_ref, vn_ref, vc_ref)
    else:
        sink_ref, q_ref, kc_ref, vc_ref, h_ref, mod_ref, wo_ref, lng_ref, lnb_ref, o_ref = refs
        k_parts = (kc_ref,)
        v_parts = (vc_ref,)
    n_ctx = kc_ref.shape[0]
    lane = lax.broadcasted_iota(jnp.int32, (1, V7X_LANES), 1)
    lo = lane < AT_HEAD_DIM
    first_head = lax.broadcasted_iota(jnp.int32, (2 * tq, 1), 0) < tq

    if window:
        start = (pl.program_id(0) % (seq // tq)) * tq
        n_win = tq + 2 * AT_WINDOW
        q_pos = start + lax.broadcasted_iota(jnp.int32, (2 * tq, 1), 0) % tq
        k_idx = lax.broadcasted_iota(jnp.int32, (1, n_win + n_ctx), 1)
        k_pos = start - AT_WINDOW + k_idx
        valid = ((jnp.abs(q_pos - k_pos) <= AT_WINDOW) & (k_pos >= 0) & (k_pos < seq)) | (k_idx >= n_win)

    pairs = []
    for p in range(AT_HEADS // 2):
        grp = (2 * p) // (AT_HEADS // AT_KV_HEADS)
        cols = slice(grp * V7X_LANES, (grp + 1) * V7X_LANES)
        qp = q_ref[:, p * V7X_LANES:(p + 1) * V7X_LANES]
        zero = jnp.zeros_like(qp)
        qs = jnp.concatenate([jnp.where(lo, qp, zero), jnp.where(lo, zero, qp)], axis=0)
        kc = jnp.concatenate([r[:, cols] for r in k_parts], axis=0)
        vc = jnp.concatenate([r[:, cols] for r in v_parts], axis=0)
        s = lax.dot_general(qs, kc, (((1,), (1,)), ((), ())), preferred_element_type=F32)
        if window:
            s = jnp.where(valid, s, -jnp.inf)
        sk = jnp.where(first_head, sink_ref[2 * p], sink_ref[2 * p + 1])
        mx = jnp.maximum(jnp.max(s, axis=-1, keepdims=True), sk)
        e = jnp.exp(s - mx)
        den = jnp.sum(e, axis=-1, keepdims=True) + jnp.exp(sk - mx)
        o2 = _dot(e.astype(BF16), vc) / den
        pairs.append(jnp.where(lo, o2[:tq], o2[tq:]))
    o = jnp.concatenate(pairs, axis=1).astype(BF16)
    y = _dot(o, wo_ref[...])
    h = h_ref[...]
    gate = mod_ref[pl.ds(5, 1), :]
    o_ref[...] = _post_norm(h, y, gate, 1.0, lng_ref[...], lnb_ref[...], alpha)


def _attn(h, st, mod, sink, q, k, v, kctx, vctx, wo, lng, lnb, alpha, n_ctx, window):
    rows, d = h.shape
    nq, nk = q.shape[1], kctx.shape[1]
    tq = ROW_TILE
    tiles_per_seq = st.seq // tq
    smem = pl.BlockSpec(memory_space=pltpu.SMEM)
    ctx_spec = pl.BlockSpec((n_ctx, nk), lambda i: (i // tiles_per_seq, 0))
    if window:
        w_blk = AT_WINDOW
        per_tile = tq // w_blk
        last = rows // w_blk - 1
        per_seq = st.seq // w_blk

        def prev_map(i):
            return (jnp.maximum(i * per_tile - 1, 0), 0)

        def next_map(i):
            return (jnp.minimum((i + 1) * per_tile, last), 0)

        kv_specs = [pl.BlockSpec((w_blk, nk), prev_map), _row_spec(tq, nk), pl.BlockSpec((w_blk, nk), next_map),
                    ctx_spec]
        in_specs = [smem, _row_spec(tq, nq)] + kv_specs + kv_specs
        args = [sink, q, k, k, k, kctx, v, v, v, vctx]
        del per_seq
    else:
        in_specs = [smem, _row_spec(tq, nq), ctx_spec, ctx_spec]
        args = [sink, q, kctx, vctx]
    in_specs += [_row_spec(tq, d), st.mod_spec(tq, d), _resident(wo.shape), _resident((1, d)), _resident((1, d))]
    args += [h, mod, wo, lng, lnb]
    return pl.pallas_call(
        functools.partial(_attn_kernel, window=window, tq=tq, seq=st.seq, alpha=alpha),
        grid=(rows // tq,),
        in_specs=in_specs,
        out_specs=_row_spec(tq, d),
        out_shape=jax.ShapeDtypeStruct((rows, d), F32),
        compiler_params=_cparams("parallel"),
        name="attn_window" if window else "attn_context",
    )(*args)


def _attn_weights(w_qkv):
    d = AT_HEAD_DIM
    deint = np.concatenate([np.arange(0, d, 2), np.arange(1, d, 2)])
    q_cols = np.concatenate([h * d + deint for h in range(AT_HEADS)])
    k0 = AT_HEADS * d
    k_cols = np.concatenate([np.tile(k0 + g * d + deint, 2) for g in range(AT_KV_HEADS)])
    v0 = k0 + AT_KV_HEADS * d
    v_cols = np.concatenate([np.tile(v0 + g * d + np.arange(d), 2) for g in range(AT_KV_HEADS)])
    return w_qkv[:, np.concatenate([q_cols, k_cols, v_cols])].astype(BF16)


def _ml_prep_kernel(hp_ref, h_ref, hn_ref, mod_ref, wxm_ref, cw_ref, cb_ref, bdqk_ref, bdv_ref,
                    wif_ref, bif_ref, q_ref, kt_ref, v_ref, xc_ref, g_ref, *, tm, seq, k_scale):
    e = xc_ref.shape[1]
    dh = e // ML_HEADS
    prev_ok, next_ok = _halo_valid(tm, seq)
    u_ext = _modulate(jnp.concatenate([hp_ref[...], h_ref[...], hn_ref[...]], axis=0), mod_ref, 1).astype(BF16)
    lo, hi = V7X_SUBLANES, V7X_SUBLANES + tm
    xm_ext = _dot(u_ext, wxm_ref[...])
    xm = xm_ext[lo:hi]
    xm_prev = xm_ext[lo - 1:lo, :] * prev_ok
    xm_next = xm_ext[hi:hi + 1, :] * next_ok
    xc = _silu(_conv3(xm_prev, xm, xm_next, cw_ref, tm) + cb_ref[...])
    xcb = xc.astype(BF16)
    xc_ref[...] = xcb
    xmb = xm.astype(BF16)
    blk = V7X_MXU_DIM
    qs, ks, vs = [], [], []
    for j in range(e // blk):
        qk = _dot(xcb[:, j * blk:(j + 1) * blk], bdqk_ref[j])
        qs.append(qk[:, :blk])
        ks.append(qk[:, blk:])
        vs.append(_dot(xmb[:, j * blk:(j + 1) * blk], bdv_ref[j]))
    q = jnp.concatenate(qs, axis=1).astype(BF16)
    k = jnp.concatenate(ks, axis=1)
    v = jnp.concatenate(vs, axis=1).astype(BF16)
    gates = _dot(jnp.concatenate([q, k.astype(BF16), v], axis=1), wif_ref[...]) + bif_ref[...]
    for hd in range(ML_HEADS):
        q_ref[hd] = q[:, hd * dh:(hd + 1) * dh]
        v_ref[hd] = v[:, hd * dh:(hd + 1) * dh]
    kt_ref[...] = (k * k_scale).T.astype(BF16)

    n_h = ML_HEADS
    lf = _log_sigmoid(gates)
    lf_hi = lf.astype(BF16)
    lf_lo = (lf - lf_hi.astype(F32)).astype(BF16)
    ri = lax.broadcasted_iota(jnp.int32, (tm, tm), 0)
    ci = lax.broadcasted_iota(jnp.int32, (tm, tm), 1)
    lower = (ci <= ri).astype(BF16)
    upper = (ci >= ri).astype(BF16)
    prefix = _dot(lower, lf_hi) + _dot(lower, lf_lo)
    suffix = _dot(upper, lf_hi) + _dot(upper, lf_lo)
    lane = lax.broadcasted_iota(jnp.int32, (1, V7X_LANES), 1)
    g_ref[...] = jnp.where((lane >= n_h) & (lane < 2 * n_h), prefix,
                           jnp.where((lane >= 3 * n_h) & (lane < 4 * n_h), suffix, gates))


def _ml_prep(h, st, mod, wxm, cw, cb, bdqk, bdv, wif, bif):
    rows, d = h.shape
    e = wxm.shape[1]
    dh = e // ML_HEADS
    tm = ML_CHUNK
    prev, nxt = _halo_specs(tm, d, rows)
    k_scale = dh ** -0.5
    heads = pl.BlockSpec((ML_HEADS, tm, dh), lambda i: (0, i, 0))
    heads_shape = jax.ShapeDtypeStruct((ML_HEADS, rows, dh), BF16)
    return pl.pallas_call(
        functools.partial(_ml_prep_kernel, tm=tm, seq=st.seq, k_scale=k_scale),
        grid=(rows // tm,),
        in_specs=[
            prev, _row_spec(tm, d), nxt, st.mod_spec(tm, d),
            _resident(wxm.shape), _resident(cw.shape), _resident(cb.shape),
            _resident(bdqk.shape), _resident(bdv.shape), _resident(wif.shape), _resident(bif.shape),
        ],
        out_specs=[heads, pl.BlockSpec((None, e, tm), lambda i: (i, 0, 0)), heads, _row_spec(tm, e),
                   _row_spec(tm, V7X_LANES)],
        out_shape=[heads_shape, jax.ShapeDtypeStruct((rows // tm, e, tm), BF16), heads_shape,
                   jax.ShapeDtypeStruct((rows, e), BF16), jax.ShapeDtypeStruct((rows, V7X_LANES), F32)],
        compiler_params=_cparams("parallel"),
        name="mlstm_prep",
    )(h, h, h, mod, wxm, cw, cb, bdqk, bdv, wif, bif)


def _log_sigmoid(x):
    return jnp.minimum(x, 0.0) - jnp.log(1.0 + jnp.exp(-jnp.abs(x)))


def _ml_chunk(q_ref, kt_ref, v_ref, gr_ref, gc_ref, out_ref, c_ref, cb_ref, n_ref, m_ref, *, head, backward, t):
    q, kt, v = q_ref[...], kt_ref[...], v_ref[...]
    i_idx = 2 * ML_HEADS * int(backward) + head
    b_idx = i_idx + ML_HEADS
    ig_row, b_row = gr_ref[pl.ds(i_idx, 1), :], gr_ref[pl.ds(b_idx, 1), :]
    lane = lax.broadcasted_iota(jnp.int32, (1, V7X_LANES), 1)
    b_col = jnp.sum(jnp.where(lane == b_idx, gc_ref[...], 0.0), axis=-1, keepdims=True)
    m_prev = m_ref[0:1, 0:1]
    g = jnp.min(b_row, axis=-1, keepdims=True)

    if out_ref is not None:
        ri = lax.broadcasted_iota(jnp.int32, (t, t), 0)
        ci = lax.broadcasted_iota(jnp.int32, (t, t), 1)
        before = (ci >= ri) if backward else (ci <= ri)
        r = jnp.where(before, ig_row - b_row, -jnp.inf)
        mx = jnp.maximum(jnp.max(r, axis=-1, keepdims=True), m_prev)
        w = _dot(q, kt) * jnp.exp(r - mx)
        w_inter = jnp.exp(m_prev - mx)
        qn = jnp.sum(q.astype(F32) * n_ref[0:1, :], axis=-1, keepdims=True)
        den = jnp.sum(w, axis=-1, keepdims=True) + w_inter * qn
        scale = 1.0 / jnp.maximum(jnp.abs(den), jnp.exp(-(b_col + mx)))
        out_ref[...] = ((_dot(w.astype(BF16), v) + w_inter * _dot(q, cb_ref[...])) * scale).astype(out_ref.dtype)

    a_row = g - b_row + ig_row
    m_new = jnp.maximum(g + m_prev, jnp.max(a_row, axis=-1, keepdims=True))
    w_s = jnp.exp(a_row - m_new).astype(BF16)
    decay = jnp.exp(g + m_prev - m_new)
    c_new = decay * c_ref[...] + _dot(kt * w_s, v)
    c_ref[...] = c_new
    cb_ref[...] = c_new.astype(BF16)
    w_s8 = jnp.broadcast_to(w_s, (V7X_SUBLANES * 2, t))
    n_ref[...] = decay * n_ref[...] + lax.dot_general(
        w_s8, kt, (((1,), (1,)), ((), ())), preferred_element_type=F32)
    m_ref[...] = jnp.broadcast_to(m_new, m_ref.shape)


def _ml_scan_kernel(*refs, ctx_out, t, heads):
    ctx_in, fwd_in, bwd_in = refs[0:5], refs[5:10], refs[10:15]
    n_out = 4 if ctx_out else 2
    outs = refs[15:15 + n_out]
    c_ref, cb_ref, n_ref, m_ref = refs[15 + n_out:]
    hf_ref, hb_ref = outs[0], outs[1]
    hcf_ref, hcb_ref = (outs[2], outs[3]) if ctx_out else (None, None)
    step = pl.program_id(2)
    dh = c_ref.shape[1]

    def run(inputs, out_ref, backward):
        q_ref, kt_ref, v_ref, gr_ref, gc_ref = inputs
        for hh in range(heads):
            slot = heads * int(backward) + hh
            head = pl.program_id(1) * heads + hh
            _ml_chunk(q_ref.at[hh], kt_ref.at[pl.ds(hh * dh, dh), :], v_ref.at[hh], gr_ref, gc_ref,
                      None if out_ref is None else out_ref.at[hh],
                      c_ref.at[slot], cb_ref.at[slot], n_ref.at[slot], m_ref.at[slot],
                      head=head, backward=backward, t=t)

    @pl.when(step == 0)
    def _():
        for ref in (c_ref, cb_ref, n_ref, m_ref):
            ref[...] = jnp.zeros_like(ref)
        run(ctx_in, hcf_ref, False)
        run(ctx_in, hcb_ref, True)

    @pl.when(step > 0)
    def _():
        run(fwd_in, hf_ref, False)
        run(bwd_in, hb_ref, True)


def _ml_scan(qkv_c, gates_c, qkv_l, gates_l, batch, n_ctx, seq, ctx_out):
    qc, ktc, vc = qkv_c
    ql, ktl, vl = qkv_l
    grc, gcc = gates_c
    grl, gcl = gates_l
    n_heads, _, dh = ql.shape
    t = ML_CHUNK
    heads = ML_SCAN_HEADS
    assert n_ctx == t, "context length must equal the mLSTM chunk length"
    ncl = seq // t
    n_gates = grl.shape[0]
    fwd = lambda b, s: b * ncl + jnp.maximum(s - 1, 0)
    bwd = lambda b, s: b * ncl + ncl - 1 - jnp.maximum(s - 1, 0)
    ctx = lambda b, s: b

    def head_rows(chunk):
        return pl.BlockSpec((heads, t, dh), lambda b, hg, s: (hg, chunk(b, s), 0))

    def chunk_specs(chunk):
        return [head_rows(chunk),
                pl.BlockSpec((None, heads * dh, t), lambda b, hg, s: (chunk(b, s), hg, 0)),
                head_rows(chunk),
                pl.BlockSpec((n_gates, t), lambda b, hg, s: (0, chunk(b, s))),
                pl.BlockSpec((t, V7X_LANES), lambda b, hg, s: (chunk(b, s), 0))]

    out_specs = [head_rows(fwd), head_rows(bwd)]
    out_shape = [jax.ShapeDtypeStruct((n_heads, batch * seq, dh), BF16)] * 2
    if ctx_out:
        out_specs += [head_rows(ctx), head_rows(ctx)]
        out_shape += [jax.ShapeDtypeStruct((n_heads, batch * n_ctx, dh), BF16)] * 2
    slots = 2 * heads
    return pl.pallas_call(
        functools.partial(_ml_scan_kernel, ctx_out=ctx_out, t=t, heads=heads),
        grid=(batch, n_heads // heads, ncl + 1),
        in_specs=chunk_specs(ctx) + chunk_specs(fwd) + chunk_specs(bwd),
        out_specs=out_specs,
        out_shape=out_shape,
        scratch_shapes=[pltpu.VMEM((slots, dh, dh), F32), pltpu.VMEM((slots, dh, dh), BF16),
                        pltpu.VMEM((slots, 2 * V7X_SUBLANES, dh), F32),
                        pltpu.VMEM((slots, V7X_SUBLANES, V7X_LANES), F32)],
        compiler_params=_cparams("parallel", "parallel", "arbitrary"),
        name="mlstm_scan",
    )(qc, ktc, vc, grc, gcc, ql, ktl, vl, grl, gcl, ql, ktl, vl, grl, gcl)


def _ml_finish_kernel(hf_ref, hb_ref, xc_ref, h_ref, mod_ref, wz_ref, ng_ref, skip_ref, wd_ref, lng_ref, lnb_ref,
                      o_ref, *, alpha):
    h = h_ref[...]
    z = _dot(_modulate(h, mod_ref, 1).astype(BF16), wz_ref[...])
    dh = hf_ref.shape[2]
    parts = []
    for hd in range(ML_HEADS):
        x = (hf_ref[hd].astype(F32) + hb_ref[hd].astype(F32)) * _sigmoid(z[:, hd * dh:(hd + 1) * dh])
        mu = jnp.mean(x, axis=-1, keepdims=True)
        xz = x - mu
        var = jnp.mean(xz * xz, axis=-1, keepdims=True)
        parts.append(xz * lax.rsqrt(var + LN_EPS))
    hn = jnp.concatenate(parts, axis=1) * ng_ref[...]
    y = _dot((hn + skip_ref[...] * xc_ref[...].astype(F32)).astype(BF16), wd_ref[...])
    gate = mod_ref[pl.ds(5, 1), :]
    o_ref[...] = _post_norm(h, y, gate, 1.0, lng_ref[...], lnb_ref[...], alpha)


def _ml_finish(hf, hb, xc, h, st, mod, wz, ng, skip, wd, lng, lnb, alpha):
    rows, d = h.shape
    n_heads, _, dh = hf.shape
    e = n_heads * dh
    tm = WIDE_ROW_TILE
    heads = pl.BlockSpec((n_heads, tm, dh), lambda i: (0, i, 0))
    return pl.pallas_call(
        functools.partial(_ml_finish_kernel, alpha=alpha),
        grid=(rows // tm,),
        in_specs=[
            heads, heads, _row_spec(tm, e), _row_spec(tm, d), st.mod_spec(tm, d), _resident(wz.shape),
            _resident((1, e)), _resident((1, e)), _resident(wd.shape), _resident((1, d)), _resident((1, d)),
        ],
        out_specs=_row_spec(tm, d),
        out_shape=jax.ShapeDtypeStruct((rows, d), F32),
        compiler_params=_cparams("parallel"),
        name="mlstm_finish",
    )(hf, hb, xc, h, mod, wz, ng, skip, wd, lng, lnb)


def _blockdiag_tiles(w):
    nblk, bs, _ = w.shape
    side = V7X_MXU_DIM
    rows = w.reshape(nblk * bs // side, side, bs)
    idx = np.arange(side) // bs
    on_diagonal = jnp.asarray(idx[:, None] == idx[None, :])
    return jnp.where(on_diagonal, jnp.tile(rows, (1, 1, side // bs)), 0.0)


def _gate_layouts(g):
    return g[:, :4 * ML_HEADS].T, g


def kernel(x, c, ctx, c_ctx, mod_w, mod_b, ln_g, ln_b, ffn_w_in, ffn_w_out, ml_w_up, ml_conv_w, ml_conv_b,
           ml_w_qkv, ml_w_if, ml_b_if, ml_skip, ml_norm_g, ml_w_down, at_w_qkv, at_sink, at_w_o, sc_w_in,
           sc_conv_w, sc_w_out):
    batch, seq, d = x.shape
    n_ctx = ctx.shape[1]
    depth = mod_w.shape[0]
    alpha = (2 * depth) ** 0.25
    assert batch + 1 <= MOD_ROWS and seq % ROW_TILE == 0 and n_ctx % ROW_TILE == 0

    lat = _Stream(batch * seq, seq, 0, seq)
    con = _Stream(batch * n_ctx, n_ctx, batch, batch * n_ctx)
    hl = x.reshape(batch * seq, d)
    hc = ctx.reshape(batch * n_ctx, d)

    cond = jnp.zeros((MOD_ROWS, d), F32).at[:batch].set(c).at[batch].set(c_ctx)
    mods = _mod_all(cond, mod_w, mod_b).reshape(depth, MOD_ROWS, N_MOD, d)

    ffn_tm = WIDE_ROW_TILE
    ffn_in, ffn_out = ffn_w_in.astype(BF16), ffn_w_out.astype(BF16)
    for i in range(depth):
        kind, j, last = i % N_MIXERS, i // N_MIXERS, i == depth - 1
        mod = mods[i]
        ln = lambda s: (ln_g[i, s][None], ln_b[i, s][None])

        def ffn_both(hl, hc, s, do_ctx):
            hl = _ffn(hl, lat, mod, ffn_in, ffn_out, i, s, *ln(2 * s), alpha, ffn_tm)
            if do_ctx:
                hc = _ffn(hc, con, mod, ffn_in, ffn_out, i, s, *ln(2 * s), alpha, ffn_tm)
            return hl, hc

        hl, hc = ffn_both(hl, hc, 0, True)

        if kind == 0:
            e = ml_w_up.shape[2] // 2
            wxm, wz = ml_w_up[j][:, :e].astype(BF16), ml_w_up[j][:, e:].astype(BF16)
            bdq, bdk, bdv = (_blockdiag_tiles(ml_w_qkv[j, a]).astype(BF16) for a in range(3))
            bdqk = jnp.concatenate([bdq, bdk], axis=2)
            n_g = 4 * ML_HEADS
            wif = jnp.concatenate([ml_w_if[j, 0], ml_w_if[j, 1]], axis=1)
            wif = jnp.pad(wif, ((0, 0), (0, V7X_LANES - n_g))).astype(BF16)
            bif = jnp.pad(jnp.concatenate([ml_b_if[j, 0], ml_b_if[j, 1]]), (0, V7X_LANES - n_g))[None]
            prep = lambda h, st: _ml_prep(h, st, mod, wxm, ml_conv_w[j], ml_conv_b[j][None], bdqk, bdv, wif, bif)
            qc, kc, vc, xcc, gc = prep(hc, con)
            ql, kl, vl, xcl, gl = prep(hl, lat)
            outs = _ml_scan((qc, kc, vc), _gate_layouts(gc), (ql, kl, vl), _gate_layouts(gl),
                            batch, n_ctx, seq, not last)
            fin = lambda hf, hb, xc, h, st: _ml_finish(
                hf, hb, xc, h, st, mod, wz, ml_norm_g[j][None], ml_skip[j][None], ml_w_down[j].astype(BF16),
                *ln(1), alpha)
            hl = fin(outs[0], outs[1], xcl, hl, lat)
            if not last:
                hc = fin(outs[2], outs[3], xcc, hc, con)
        elif kind == 1:
            w = _attn_weights(at_w_qkv[j])
            wo = at_w_o[j].astype(BF16)
            ql, kl, vl = _qkv(hl, lat, mod, w, _rope_tables(seq))
            qc, kc, vc = _qkv(hc, con, mod, w, None)
            hl = _attn(hl, lat, mod, at_sink[j], ql, kl, vl, kc, vc, wo, *ln(1), alpha, n_ctx, True)
            if not last:
                hc = _attn(hc, con, mod, at_sink[j], qc, None, None, kc, vc, wo, *ln(1), alpha, n_ctx, False)
        else:
            w = sc_w_in[j]
            wb, wcx, wo = w[:, :d].astype(BF16), w[:, d:].astype(BF16), sc_w_out[j].astype(BF16)
            hl = _sconv(hl, lat, mod, wb, wcx, sc_conv_w[j], wo, *ln(1), alpha)
            if not last:
                hc = _sconv(hc, con, mod, wb, wcx, sc_conv_w[j], wo, *ln(1), alpha)

        hl, hc = ffn_both(hl, hc, 1, not last)
    return hl.reshape(batch, seq, d)
```

```python
import functools

import jax
import jax.numpy as jnp
import numpy as np
from jax import lax
from jax.experimental import pallas as pl
from jax.experimental.pallas import tpu as pltpu

N_MIXERS = 3
N_MOD = 9
LN_EPS = 1e-5
GRID_W = 64
ML_HEADS = 4
ML_QKV_BLOCK = 4
AT_HEADS = 16
AT_KV_HEADS = 4
AT_HEAD_DIM = 64
AT_WINDOW = 128
ROPE_BASE = 10000.0
LOG2_E = 1.4426950408889634

V7X_LANES = 128
V7X_SUBLANES = 8
V7X_MXU_DIM = 256
V7X_VMEM_LIMIT_BYTES = 56 * 1024 * 1024

ROW_TILE = 256
WIDE_ROW_TILE = 512
ML_CHUNK = 256
ML_SCAN_HEADS = 4
FFN_SUB_TILES = 2
MOD_ROWS = 16
MOD_COL_TILE = 1024

BF16 = jnp.bfloat16
F32 = jnp.float32


def _cparams(*sem):
    return pltpu.CompilerParams(dimension_semantics=sem, vmem_limit_bytes=V7X_VMEM_LIMIT_BYTES)


def _resident(shape):
    nd = len(shape)
    return pl.BlockSpec(shape, lambda *_: (0,) * nd, pipeline_mode=pl.Buffered(1))


def _dot(a, b):
    return jnp.dot(a, b, preferred_element_type=F32)


def _sigmoid(x):
    return 1.0 / (1.0 + jnp.exp(-x))


def _silu(x):
    return x * _sigmoid(x)


def _modulate(h, mod_ref, slot):
    shift = mod_ref[pl.ds(3 * slot, 1), :]
    scale = mod_ref[pl.ds(3 * slot + 1, 1), :]
    return h * (1.0 + scale) + shift


def _post_norm(h, y, gate, weight, g, b, alpha):
    z = alpha * h + (weight * gate) * y
    mu = jnp.mean(z, axis=-1, keepdims=True)
    zc = z - mu
    var = jnp.mean(zc * zc, axis=-1, keepdims=True)
    return zc * lax.rsqrt(var + LN_EPS) * g + b


def _mod_kernel(cond_ref, w_ref, b_ref, o_ref):
    cond = _silu(cond_ref[...]).astype(BF16)
    o_ref[...] = _dot(cond, w_ref[...].astype(BF16)) + b_ref[...]


def _mod_all(cond, mod_w, mod_b):
    depth, d, n = mod_w.shape
    tn = MOD_COL_TILE
    return pl.pallas_call(
        _mod_kernel,
        grid=(depth, n // tn),
        in_specs=[
            pl.BlockSpec((MOD_ROWS, d), lambda i, j: (0, 0)),
            pl.BlockSpec((None, d, tn), lambda i, j: (i, 0, j)),
            pl.BlockSpec((None, 1, tn), lambda i, j: (i, 0, j)),
        ],
        out_specs=pl.BlockSpec((None, MOD_ROWS, tn), lambda i, j: (i, 0, j)),
        out_shape=jax.ShapeDtypeStruct((depth, MOD_ROWS, n), F32),
        compiler_params=_cparams("parallel", "parallel"),
        name="mod_vectors",
    )(cond, mod_w, mod_b.reshape(depth, 1, n))


class _Stream:
    def __init__(self, rows, seq, mod_base, rows_per_mod):
        self.rows = rows
        self.seq = seq
        self.mod_base = mod_base
        self.rows_per_mod = rows_per_mod

    def mod_spec(self, tm, d):
        base, rpm = self.mod_base, self.rows_per_mod
        return pl.BlockSpec((None, N_MOD, d), lambda i: (base + (i * tm) // rpm, 0, 0))


def _row_spec(tm, width):
    return pl.BlockSpec((tm, width), lambda i: (i, 0))


def _ffn_kernel(h_ref, mod_ref, wg_ref, wv_ref, wo_ref, lng_ref, lnb_ref, o_ref, *, slot, alpha):
    gate = mod_ref[pl.ds(3 * slot + 2, 1), :]
    sub = h_ref.shape[0] // FFN_SUB_TILES
    for r in range(FFN_SUB_TILES):
        rows = pl.ds(r * sub, sub)
        h = h_ref[rows, :]
        u = _modulate(h, mod_ref, slot).astype(BF16)
        g = _dot(u, wg_ref[...])
        v = _dot(u, wv_ref[...])
        a = (_silu(g) * v).astype(BF16)
        y = _dot(a, wo_ref[...])
        o_ref[rows, :] = _post_norm(h, y, gate, 0.5, lng_ref[...], lnb_ref[...], alpha)


def _ffn(h, st, mod, w_in, w_out, layer, half, lng, lnb, alpha, tm):
    rows, d = h.shape
    f = w_out.shape[2]
    slot = 2 * half

    def resident4(shape, col):
        return pl.BlockSpec((None, None) + shape, lambda i: (layer, half, 0, col), pipeline_mode=pl.Buffered(1))

    return pl.pallas_call(
        functools.partial(_ffn_kernel, slot=slot, alpha=alpha),
        grid=(rows // tm,),
        in_specs=[
            _row_spec(tm, d),
            st.mod_spec(tm, d),
            resident4((d, f), 0),
            resident4((d, f), 1),
            resident4((f, d), 0),
            _resident((1, d)),
            _resident((1, d)),
        ],
        out_specs=_row_spec(tm, d),
        out_shape=jax.ShapeDtypeStruct((rows, d), F32),
        compiler_params=_cparams("parallel"),
        name="ffn_swiglu",
    )(h, mod, w_in, w_in, w_out, lng, lnb)


def _halo_specs(tm, width, rows):
    r8 = tm // V7X_SUBLANES
    last8 = rows // V7X_SUBLANES - 1
    prev = pl.BlockSpec((V7X_SUBLANES, width), lambda i: (jnp.maximum(i * r8 - 1, 0), 0))
    nxt = pl.BlockSpec((V7X_SUBLANES, width), lambda i: (jnp.minimum((i + 1) * r8, last8), 0))
    return prev, nxt


def _halo_valid(tm, seq):
    tiles_per_seq = seq // tm
    pos = pl.program_id(0) % tiles_per_seq
    prev_ok = jnp.where(pos == 0, 0.0, 1.0).astype(F32)
    next_ok = jnp.where(pos == tiles_per_seq - 1, 0.0, 1.0).astype(F32)
    return prev_ok, next_ok


def _conv3(t_prev_row, t_main, t_next_row, w_ref, tm):
    row = lax.broadcasted_iota(jnp.int32, (tm, 1), 0)
    up = jnp.where(row == 0, t_prev_row, pltpu.roll(t_main, 1, axis=0))
    down = jnp.where(row == tm - 1, t_next_row, pltpu.roll(t_main, tm - 1, axis=0))
    return w_ref[0:1, :] * up + w_ref[1:2, :] * t_main + w_ref[2:3, :] * down


def _sconv_kernel(hp_ref, h_ref, hn_ref, mod_ref, wb_ref, wcx_ref, cw_ref, wo_ref, lng_ref, lnb_ref,
                  o_ref, *, alpha, tm, seq):
    d = h_ref.shape[1]
    h = h_ref[...]
    prev_ok, next_ok = _halo_valid(tm, seq)
    u_ext = _modulate(jnp.concatenate([hp_ref[...], h, hn_ref[...]], axis=0), mod_ref, 1).astype(BF16)
    lo, hi = V7X_SUBLANES, V7X_SUBLANES + tm
    bg = _dot(_modulate(h, mod_ref, 1).astype(BF16), wb_ref[...])
    cx = _dot(u_ext, wcx_ref[...])
    t_ext = cx[:, :d] * cx[:, d:]
    t_prev = t_ext[lo - 1:lo, :] * prev_ok
    t_next = t_ext[hi:hi + 1, :] * next_ok
    conv = _conv3(t_prev, t_ext[lo:hi], t_next, cw_ref, tm)
    y = _dot((bg * conv).astype(BF16), wo_ref[...])
    gate = mod_ref[pl.ds(5, 1), :]
    o_ref[...] = _post_norm(h, y, gate, 1.0, lng_ref[...], lnb_ref[...], alpha)


def _sconv(h, st, mod, wb, wcx, cw, wo, lng, lnb, alpha):
    rows, d = h.shape
    tm = min(WIDE_ROW_TILE, st.seq)
    prev, nxt = _halo_specs(tm, d, rows)
    return pl.pallas_call(
        functools.partial(_sconv_kernel, alpha=alpha, tm=tm, seq=st.seq),
        grid=(rows // tm,),
        in_specs=[
            prev, _row_spec(tm, d), nxt,
            st.mod_spec(tm, d),
            _resident(wb.shape), _resident(wcx.shape), _resident(cw.shape), _resident(wo.shape),
            _resident((1, d)), _resident((1, d)),
        ],
        out_specs=_row_spec(tm, d),
        out_shape=jax.ShapeDtypeStruct((rows, d), F32),
        compiler_params=_cparams("parallel"),
        name="short_conv_mixer",
    )(h, h, h, mod, wb, wcx, cw, wo, lng, lnb)


def _rope_tables(n_tokens):
    rows_n = n_tokens // GRID_W
    rows = jnp.repeat(jnp.arange(rows_n), GRID_W).astype(F32)
    cols = jnp.tile(jnp.arange(GRID_W), rows_n).astype(F32)
    axis_dim = AT_HEAD_DIM // 2
    freqs = ROPE_BASE ** (-jnp.arange(0, axis_dim, 2, dtype=F32) / axis_dim)
    ang = jnp.concatenate([rows[:, None] * freqs, cols[:, None] * freqs], axis=-1)
    cos, sin = jnp.cos(ang), jnp.sin(ang)
    cos_t = jnp.tile(jnp.concatenate([cos, cos], axis=-1), (1, 2))
    sin_t = jnp.tile(jnp.concatenate([-sin, sin], axis=-1), (1, 2))
    return cos_t, sin_t


def _swap_halves(x):
    width = x.shape[1]
    half = AT_HEAD_DIM // 2
    lane = lax.broadcasted_iota(jnp.int32, (1, width), 1)
    lo = (lane % AT_HEAD_DIM) < half
    return jnp.where(lo, pltpu.roll(x, width - half, axis=1), pltpu.roll(x, half, axis=1))


def _qkv_kernel(*refs, rope):
    if rope:
        h_ref, mod_ref, w_ref, cos_ref, sin_ref, q_ref, k_ref, v_ref = refs
    else:
        h_ref, mod_ref, w_ref, q_ref, k_ref, v_ref = refs
    nq, nk = q_ref.shape[1], k_ref.shape[1]
    u = _modulate(h_ref[...], mod_ref, 1).astype(BF16)
    p = _dot(u, w_ref[...])
    q = p[:, :nq] * (AT_HEAD_DIM ** -0.5 * LOG2_E)
    k = p[:, nq:nq + nk]
    if rope:
        cos, sin = cos_ref[...], sin_ref[...]
        q = q * jnp.tile(cos, (1, nq // V7X_LANES)) + _swap_halves(q) * jnp.tile(sin, (1, nq // V7X_LANES))
        k = k * jnp.tile(cos, (1, nk // V7X_LANES)) + _swap_halves(k) * jnp.tile(sin, (1, nk // V7X_LANES))
    q_ref[...] = q.astype(BF16)
    k_ref[...] = k.astype(BF16)
    v_ref[...] = p[:, nq + nk:].astype(BF16)


def _qkv(h, st, mod, w, tables):
    rows, d = h.shape
    tm = min(WIDE_ROW_TILE, st.seq)
    nq = AT_HEADS * AT_HEAD_DIM
    nk = AT_KV_HEADS * 2 * AT_HEAD_DIM
    rope = tables is not None
    in_specs = [_row_spec(tm, d), st.mod_spec(tm, d), _resident(w.shape)]
    args = [h, mod, w]
    if rope:
        tiles_per_seq = st.seq // tm
        tab_spec = pl.BlockSpec((tm, V7X_LANES), lambda i: (i % tiles_per_seq, 0))
        in_specs += [tab_spec, tab_spec]
        args += list(tables)
    return pl.pallas_call(
        functools.partial(_qkv_kernel, rope=rope),
        grid=(rows // tm,),
        in_specs=in_specs,
        out_specs=[_row_spec(tm, nq), _row_spec(tm, nk), _row_spec(tm, nk)],
        out_shape=[jax.ShapeDtypeStruct((rows, nq), BF16),
                   jax.ShapeDtypeStruct((rows, nk), BF16),
                   jax.ShapeDtypeStruct((rows, nk), BF16)],
        compiler_params=_cparams("parallel"),
        name="attn_qkv_rope",
    )(*args)


def _attn_kernel(*refs, window, tq, seq, alpha):
    if window:
        (sink_ref, q_ref, kp_ref, km_ref, kn_ref, kc_ref, vp_ref, vm_ref, vn_ref, vc_ref,
         h_ref, mod_ref, wo_ref, lng_ref, lnb_ref, o_ref) = refs
        k_parts = (kp_ref, km_ref, kn_ref, kc_ref)
        v_parts = (vp_ref, vm_ref, vn_ref, vc_ref)
    else:
        sink_ref, q_ref, kc_ref, vc_ref, h_ref, mod_ref, wo_ref, lng_ref, lnb_ref, o_ref = refs
        k_parts = (kc_ref,)
        v_parts = (vc_ref,)
    n_ctx = kc_ref.shape[0]
    lane = lax.broadcasted_iota(jnp.int32, (1, V7X_LANES), 1)
    lo = lane < AT_HEAD_DIM
    first_head = lax.broadcasted_iota(jnp.int32, (2 * tq, 1), 0) < tq

    if window:
        start = (pl.program_id(0) % (seq // tq)) * tq
        n_win = tq + 2 * AT_WINDOW
        q_pos = start + lax.broadcasted_iota(jnp.int32, (2 * tq, 1), 0) % tq
        k_idx = lax.broadcasted_iota(jnp.int32, (1, n_win + n_ctx), 1)
        k_pos = start - AT_WINDOW + k_idx
        valid = ((jnp.abs(q_pos - k_pos) <= AT_WINDOW) & (k_pos >= 0) & (k_pos < seq)) | (k_idx >= n_win)

    pairs = []
    for p in range(AT_HEADS // 2):
        grp = (2 * p) // (AT_HEADS // AT_KV_HEADS)
        cols = slice(grp * V7X_LANES, (grp + 1) * V7X_LANES)
        qp = q_ref[:, p * V7X_LANES:(p + 1) * V7X_LANES]
        zero = jnp.zeros_like(qp)
        qs = jnp.concatenate([jnp.where(lo, qp, zero), jnp.where(lo, zero, qp)], axis=0)
        kc = jnp.concatenate([r[:, cols] for r in k_parts], axis=0)
        vc = jnp.concatenate([r[:, cols] for r in v_parts], axis=0)
        s = lax.dot_general(qs, kc, (((1,), (1,)), ((), ())), preferred_element_type=F32)
        if window:
            s = jnp.where(valid, s, -jnp.inf)
        sk = jnp.where(first_head, sink_ref[2 * p], sink_ref[2 * p + 1]) * LOG2_E
        mx = jnp.maximum(jnp.max(s, axis=-1, keepdims=True), sk)
        e = jnp.exp2(s - mx)
        den = jnp.sum(e, axis=-1, keepdims=True) + jnp.exp2(sk - mx)
        o2 = _dot(e.astype(BF16), vc) * (1.0 / den)
        pairs.append(jnp.where(lo, o2[:tq], o2[tq:]))
    o = jnp.concatenate(pairs, axis=1).astype(BF16)
    y = _dot(o, wo_ref[...])
    h = h_ref[...]
    gate = mod_ref[pl.ds(5, 1), :]
    o_ref[...] = _post_norm(h, y, gate, 1.0, lng_ref[...], lnb_ref[...], alpha)


def _attn(h, st, mod, sink, q, k, v, kctx, vctx, wo, lng, lnb, alpha, n_ctx, window):
    rows, d = h.shape
    nq, nk = q.shape[1], kctx.shape[1]
    tq = ROW_TILE
    tiles_per_seq = st.seq // tq
    smem = pl.BlockSpec(memory_space=pltpu.SMEM)
    ctx_spec = pl.BlockSpec((n_ctx, nk), lambda i: (i // tiles_per_seq, 0))
    if window:
        w_blk = AT_WINDOW
        per_tile = tq // w_blk
        last = rows // w_blk - 1
        per_seq = st.seq // w_blk

        def prev_map(i):
            return (jnp.maximum(i * per_tile - 1, 0), 0)

        def next_map(i):
            return (jnp.minimum((i + 1) * per_tile, last), 0)

        kv_specs = [pl.BlockSpec((w_blk, nk), prev_map), _row_spec(tq, nk), pl.BlockSpec((w_blk, nk), next_map),
                    ctx_spec]
        in_specs = [smem, _row_spec(tq, nq)] + kv_specs + kv_specs
        args = [sink, q, k, k, k, kctx, v, v, v, vctx]
        del per_seq
    else:
        in_specs = [smem, _row_spec(tq, nq), ctx_spec, ctx_spec]
        args = [sink, q, kctx, vctx]
    in_specs += [_row_spec(tq, d), st.mod_spec(tq, d), _resident(wo.shape), _resident((1, d)), _resident((1, d))]
    args += [h, mod, wo, lng, lnb]
    return pl.pallas_call(
        functools.partial(_attn_kernel, window=window, tq=tq, seq=st.seq, alpha=alpha),
        grid=(rows // tq,),
        in_specs=in_specs,
        out_specs=_row_spec(tq, d),
        out_shape=jax.ShapeDtypeStruct((rows, d), F32),
        compiler_params=_cparams("parallel"),
        name="attn_window" if window else "attn_context",
    )(*args)


def _attn_weights(w_qkv):
    d = AT_HEAD_DIM
    deint = np.concatenate([np.arange(0, d, 2), np.arange(1, d, 2)])
    q_cols = np.concatenate([h * d + deint for h in range(AT_HEADS)])
    k0 = AT_HEADS * d
    k_cols = np.concatenate([np.tile(k0 + g * d + deint, 2) for g in range(AT_KV_HEADS)])
    v0 = k0 + AT_KV_HEADS * d
    v_cols = np.concatenate([np.tile(v0 + g * d + np.arange(d), 2) for g in range(AT_KV_HEADS)])
    return w_qkv[:, np.concatenate([q_cols, k_cols, v_cols])].astype(BF16)


def _ml_prep_kernel(hp_ref, h_ref, hn_ref, mod_ref, wxm_ref, cw_ref, cb_ref, bdqk_ref, bdv_ref,
                    wif_ref, bif_ref, q_ref, kt_ref, v_ref, xc_ref, g_ref, *, tm, seq, k_scale):
    e = xc_ref.shape[1]
    dh = e // ML_HEADS
    prev_ok, next_ok = _halo_valid(tm, seq)
    u_ext = _modulate(jnp.concatenate([hp_ref[...], h_ref[...], hn_ref[...]], axis=0), mod_ref, 1).astype(BF16)
    lo, hi = V7X_SUBLANES, V7X_SUBLANES + tm
    xm_ext = _dot(u_ext, wxm_ref[...])
    xm = xm_ext[lo:hi]
    xm_prev = xm_ext[lo - 1:lo, :] * prev_ok
    xm_next = xm_ext[hi:hi + 1, :] * next_ok
    xc = _silu(_conv3(xm_prev, xm, xm_next, cw_ref, tm) + cb_ref[...])
    xcb = xc.astype(BF16)
    xc_ref[...] = xcb
    xmb = xm.astype(BF16)
    blk = V7X_MXU_DIM
    qs, ks, vs = [], [], []
    for j in range(e // blk):
        qk = _dot(xcb[:, j * blk:(j + 1) * blk], bdqk_ref[j])
        qs.append(qk[:, :blk])
        ks.append(qk[:, blk:])
        vs.append(_dot(xmb[:, j * blk:(j + 1) * blk], bdv_ref[j]))
    q = jnp.concatenate(qs, axis=1).astype(BF16)
    k = jnp.concatenate(ks, axis=1)
    v = jnp.concatenate(vs, axis=1).astype(BF16)
    gates = _dot(jnp.concatenate([q, k.astype(BF16), v], axis=1), wif_ref[...]) + bif_ref[...]
    for hd in range(ML_HEADS):
        q_ref[hd] = q[:, hd * dh:(hd + 1) * dh]
        v_ref[hd] = v[:, hd * dh:(hd + 1) * dh]
    kt_ref[...] = (k * k_scale).T.astype(BF16)

    n_h = ML_HEADS
    lf = _log_sigmoid(gates)
    lf_hi = lf.astype(BF16)
    lf_lo = (lf - lf_hi.astype(F32)).astype(BF16)
    ri = lax.broadcasted_iota(jnp.int32, (tm, tm), 0)
    ci = lax.broadcasted_iota(jnp.int32, (tm, tm), 1)
    lower = (ci <= ri).astype(BF16)
    upper = (ci >= ri).astype(BF16)
    prefix = _dot(lower, lf_hi) + _dot(lower, lf_lo)
    suffix = _dot(upper, lf_hi) + _dot(upper, lf_lo)
    lane = lax.broadcasted_iota(jnp.int32, (1, V7X_LANES), 1)
    g_ref[...] = jnp.where((lane >= n_h) & (lane < 2 * n_h), prefix,
                           jnp.where((lane >= 3 * n_h) & (lane < 4 * n_h), suffix, gates))


def _ml_prep(h, st, mod, wxm, cw, cb, bdqk, bdv, wif, bif):
    rows, d = h.shape
    e = wxm.shape[1]
    dh = e // ML_HEADS
    tm = ML_CHUNK
    prev, nxt = _halo_specs(tm, d, rows)
    k_scale = dh ** -0.5
    heads = pl.BlockSpec((ML_HEADS, tm, dh), lambda i: (0, i, 0))
    heads_shape = jax.ShapeDtypeStruct((ML_HEADS, rows, dh), BF16)
    return pl.pallas_call(
        functools.partial(_ml_prep_kernel, tm=tm, seq=st.seq, k_scale=k_scale),
        grid=(rows // tm,),
        in_specs=[
            prev, _row_spec(tm, d), nxt, st.mod_spec(tm, d),
            _resident(wxm.shape), _resident(cw.shape), _resident(cb.shape),
            _resident(bdqk.shape), _resident(bdv.shape), _resident(wif.shape), _resident(bif.shape),
        ],
        out_specs=[heads, pl.BlockSpec((None, e, tm), lambda i: (i, 0, 0)), heads, _row_spec(tm, e),
                   _row_spec(tm, V7X_LANES)],
        out_shape=[heads_shape, jax.ShapeDtypeStruct((rows // tm, e, tm), BF16), heads_shape,
                   jax.ShapeDtypeStruct((rows, e), BF16), jax.ShapeDtypeStruct((rows, V7X_LANES), F32)],
        compiler_params=_cparams("parallel"),
        name="mlstm_prep",
    )(h, h, h, mod, wxm, cw, cb, bdqk, bdv, wif, bif)


def _log_sigmoid(x):
    return jnp.minimum(x, 0.0) - jnp.log(1.0 + jnp.exp(-jnp.abs(x)))


def _ml_chunk(q_ref, kt_ref, v_ref, gr_ref, gc_ref, out_ref, c_ref, cb_ref, n_ref, m_ref, *, head, backward, t):
    q, kt, v = q_ref[...], kt_ref[...], v_ref[...]
    i_idx = 2 * ML_HEADS * int(backward) + head
    b_idx = i_idx + ML_HEADS
    ig_row, b_row = gr_ref[pl.ds(i_idx, 1), :], gr_ref[pl.ds(b_idx, 1), :]
    lane = lax.broadcasted_iota(jnp.int32, (1, V7X_LANES), 1)
    b_col = jnp.sum(jnp.where(lane == b_idx, gc_ref[...], 0.0), axis=-1, keepdims=True)
    m_prev = m_ref[0:1, 0:1]
    g = jnp.min(b_row, axis=-1, keepdims=True)

    if out_ref is not None:
        ri = lax.broadcasted_iota(jnp.int32, (t, t), 0)
        ci = lax.broadcasted_iota(jnp.int32, (t, t), 1)
        before = (ci >= ri) if backward else (ci <= ri)
        r = jnp.where(before, ig_row - b_row, -jnp.inf)
        mx = jnp.maximum(jnp.max(r, axis=-1, keepdims=True), m_prev)
        w = _dot(q, kt) * jnp.exp(r - mx)
        w_inter = jnp.exp(m_prev - mx)
        qn = jnp.sum(q.astype(F32) * n_ref[0:1, :], axis=-1, keepdims=True)
        den = jnp.sum(w, axis=-1, keepdims=True) + w_inter * qn
        scale = 1.0 / jnp.maximum(jnp.abs(den), jnp.exp(-(b_col + mx)))
        out_ref[...] = ((_dot(w.astype(BF16), v) + w_inter * _dot(q, cb_ref[...])) * scale).astype(out_ref.dtype)

    a_row = g - b_row + ig_row
    m_new = jnp.maximum(g + m_prev, jnp.max(a_row, axis=-1, keepdims=True))
    w_s = jnp.exp(a_row - m_new).astype(BF16)
    decay = jnp.exp(g + m_prev - m_new)
    c_new = decay * c_ref[...] + _dot(kt * w_s, v)
    c_ref[...] = c_new
    cb_ref[...] = c_new.astype(BF16)
    w_s8 = jnp.broadcast_to(w_s, (V7X_SUBLANES * 2, t))
    n_ref[...] = decay * n_ref[...] + lax.dot_general(
        w_s8, kt, (((1,), (1,)), ((), ())), preferred_element_type=F32)
    m_ref[...] = jnp.broadcast_to(m_new, m_ref.shape)


def _ml_scan_kernel(*refs, ctx_out, t, heads):
    ctx_in, fwd_in, bwd_in = refs[0:5], refs[5:10], refs[10:15]
    n_out = 4 if ctx_out else 2
    outs = refs[15:15 + n_out]
    c_ref, cb_ref, n_ref, m_ref = refs[15 + n_out:]
    hf_ref, hb_ref = outs[0], outs[1]
    hcf_ref, hcb_ref = (outs[2], outs[3]) if ctx_out else (None, None)
    step = pl.program_id(2)
    dh = c_ref.shape[1]

    def run(inputs, out_ref, backward):
        q_ref, kt_ref, v_ref, gr_ref, gc_ref = inputs
        for hh in range(heads):
            slot = heads * int(backward) + hh
            head = pl.program_id(1) * heads + hh
            _ml_chunk(q_ref.at[hh], kt_ref.at[pl.ds(hh * dh, dh), :], v_ref.at[hh], gr_ref, gc_ref,
                      None if out_ref is None else out_ref.at[hh],
                      c_ref.at[slot], cb_ref.at[slot], n_ref.at[slot], m_ref.at[slot],
                      head=head, backward=backward, t=t)

    @pl.when(step == 0)
    def _():
        for ref in (c_ref, cb_ref, n_ref, m_ref):
            ref[...] = jnp.zeros_like(ref)
        run(ctx_in, hcf_ref, False)
        run(ctx_in, hcb_ref, True)

    @pl.when(step > 0)
    def _():
        run(fwd_in, hf_ref, False)
        run(bwd_in, hb_ref, True)


def _ml_scan(qkv_c, gates_c, qkv_l, gates_l, batch, n_ctx, seq, ctx_out):
    qc, ktc, vc = qkv_c
    ql, ktl, vl = qkv_l
    grc, gcc = gates_c
    grl, gcl = gates_l
    n_heads, _, dh = ql.shape
    t = ML_CHUNK
    heads = ML_SCAN_HEADS
    assert n_ctx == t, "context length must equal the mLSTM chunk length"
    ncl = seq // t
    n_gates = grl.shape[0]
    fwd = lambda b, s: b * ncl + jnp.maximum(s - 1, 0)
    bwd = lambda b, s: b * ncl + ncl - 1 - jnp.maximum(s - 1, 0)
    ctx = lambda b, s: b

    def head_rows(chunk):
        return pl.BlockSpec((heads, t, dh), lambda b, hg, s: (hg, chunk(b, s), 0))

    def chunk_specs(chunk):
        return [head_rows(chunk),
                pl.BlockSpec((None, heads * dh, t), lambda b, hg, s: (chunk(b, s), hg, 0)),
                head_rows(chunk),
                pl.BlockSpec((n_gates, t), lambda b, hg, s: (0, chunk(b, s))),
                pl.BlockSpec((t, V7X_LANES), lambda b, hg, s: (chunk(b, s), 0))]

    out_specs = [head_rows(fwd), head_rows(bwd)]
    out_shape = [jax.ShapeDtypeStruct((n_heads, batch * seq, dh), BF16)] * 2
    if ctx_out:
        out_specs += [head_rows(ctx), head_rows(ctx)]
        out_shape += [jax.ShapeDtypeStruct((n_heads, batch * n_ctx, dh), BF16)] * 2
    slots = 2 * heads
    return pl.pallas_call(
        functools.partial(_ml_scan_kernel, ctx_out=ctx_out, t=t, heads=heads),
        grid=(batch, n_heads // heads, ncl + 1),
        in_specs=chunk_specs(ctx) + chunk_specs(fwd) + chunk_specs(bwd),
        out_specs=out_specs,
        out_shape=out_shape,
        scratch_shapes=[pltpu.VMEM((slots, dh, dh), F32), pltpu.VMEM((slots, dh, dh), BF16),
                        pltpu.VMEM((slots, 2 * V7X_SUBLANES, dh), F32),
                        pltpu.VMEM((slots, V7X_SUBLANES, V7X_LANES), F32)],
        compiler_params=_cparams("parallel", "parallel", "arbitrary"),
        name="mlstm_scan",
    )(qc, ktc, vc, grc, gcc, ql, ktl, vl, grl, gcl, ql, ktl, vl, grl, gcl)


def _ml_finish_kernel(hf_ref, hb_ref, xc_ref, h_ref, mod_ref, wz_ref, ng_ref, skip_ref, wd_ref, lng_ref, lnb_ref,
                      o_ref, *, alpha):
    h = h_ref[...]
    z = _dot(_modulate(h, mod_ref, 1).astype(BF16), wz_ref[...])
    dh = hf_ref.shape[2]
    parts = []
    for hd in range(ML_HEADS):
        x = (hf_ref[hd].astype(F32) + hb_ref[hd].astype(F32)) * _sigmoid(z[:, hd * dh:(hd + 1) * dh])
        mu = jnp.mean(x, axis=-1, keepdims=True)
        xz = x - mu
        var = jnp.mean(xz * xz, axis=-1, keepdims=True)
        parts.append(xz * lax.rsqrt(var + LN_EPS))
    hn = jnp.concatenate(parts, axis=1) * ng_ref[...]
    y = _dot((hn + skip_ref[...] * xc_ref[...].astype(F32)).astype(BF16), wd_ref[...])
    gate = mod_ref[pl.ds(5, 1), :]
    o_ref[...] = _post_norm(h, y, gate, 1.0, lng_ref[...], lnb_ref[...], alpha)


def _ml_finish(hf, hb, xc, h, st, mod, wz, ng, skip, wd, lng, lnb, alpha):
    rows, d = h.shape
    n_heads, _, dh = hf.shape
    e = n_heads * dh
    tm = WIDE_ROW_TILE
    heads = pl.BlockSpec((n_heads, tm, dh), lambda i: (0, i, 0))
    return pl.pallas_call(
        functools.partial(_ml_finish_kernel, alpha=alpha),
        grid=(rows // tm,),
        in_specs=[
            heads, heads, _row_spec(tm, e), _row_spec(tm, d), st.mod_spec(tm, d), _resident(wz.shape),
            _resident((1, e)), _resident((1, e)), _resident(wd.shape), _resident((1, d)), _resident((1, d)),
        ],
        out_specs=_row_spec(tm, d),
        out_shape=jax.ShapeDtypeStruct((rows, d), F32),
        compiler_params=_cparams("parallel"),
        name="mlstm_finish",
    )(hf, hb, xc, h, mod, wz, ng, skip, wd, lng, lnb)


def _blockdiag_tiles(w):
    nblk, bs, _ = w.shape
    side = V7X_MXU_DIM
    rows = w.reshape(nblk * bs // side, side, bs)
    idx = np.arange(side) // bs
    on_diagonal = jnp.asarray(idx[:, None] == idx[None, :])
    return jnp.where(on_diagonal, jnp.tile(rows, (1, 1, side // bs)), 0.0)


def _gate_layouts(g):
    return g[:, :4 * ML_HEADS].T, g


def kernel(x, c, ctx, c_ctx, mod_w, mod_b, ln_g, ln_b, ffn_w_in, ffn_w_out, ml_w_up, ml_conv_w, ml_conv_b,
           ml_w_qkv, ml_w_if, ml_b_if, ml_skip, ml_norm_g, ml_w_down, at_w_qkv, at_sink, at_w_o, sc_w_in,
           sc_conv_w, sc_w_out):
    batch, seq, d = x.shape
    n_ctx = ctx.shape[1]
    depth = mod_w.shape[0]
    alpha = (2 * depth) ** 0.25
    assert batch + 1 <= MOD_ROWS and seq % ROW_TILE == 0 and n_ctx % ROW_TILE == 0

    lat = _Stream(batch * seq, seq, 0, seq)
    con = _Stream(batch * n_ctx, n_ctx, batch, batch * n_ctx)
    hl = x.reshape(batch * seq, d)
    hc = ctx.reshape(batch * n_ctx, d)

    cond = jnp.zeros((MOD_ROWS, d), F32).at[:batch].set(c).at[batch].set(c_ctx)
    mods = _mod_all(cond, mod_w, mod_b).reshape(depth, MOD_ROWS, N_MOD, d)

    ffn_tm = WIDE_ROW_TILE
    ffn_in, ffn_out = ffn_w_in.astype(BF16), ffn_w_out.astype(BF16)
    for i in range(depth):
        kind, j, last = i % N_MIXERS, i // N_MIXERS, i == depth - 1
        mod = mods[i]
        ln = lambda s: (ln_g[i, s][None], ln_b[i, s][None])

        def ffn_both(hl, hc, s, do_ctx):
            hl = _ffn(hl, lat, mod, ffn_in, ffn_out, i, s, *ln(2 * s), alpha, ffn_tm)
            if do_ctx:
                hc = _ffn(hc, con, mod, ffn_in, ffn_out, i, s, *ln(2 * s), alpha, ffn_tm)
            return hl, hc

        hl, hc = ffn_both(hl, hc, 0, True)

        if kind == 0:
            e = ml_w_up.shape[2] // 2
            wxm, wz = ml_w_up[j][:, :e].astype(BF16), ml_w_up[j][:, e:].astype(BF16)
            bdq, bdk, bdv = (_blockdiag_tiles(ml_w_qkv[j, a]).astype(BF16) for a in range(3))
            bdqk = jnp.concatenate([bdq, bdk], axis=2)
            n_g = 4 * ML_HEADS
            wif = jnp.concatenate([ml_w_if[j, 0], ml_w_if[j, 1]], axis=1)
            wif = jnp.pad(wif, ((0, 0), (0, V7X_LANES - n_g))).astype(BF16)
            bif = jnp.pad(jnp.concatenate([ml_b_if[j, 0], ml_b_if[j, 1]]), (0, V7X_LANES - n_g))[None]
            prep = lambda h, st: _ml_prep(h, st, mod, wxm, ml_conv_w[j], ml_conv_b[j][None], bdqk, bdv, wif, bif)
            qc, kc, vc, xcc, gc = prep(hc, con)
            ql, kl, vl, xcl, gl = prep(hl, lat)
            outs = _ml_scan((qc, kc, vc), _gate_layouts(gc), (ql, kl, vl), _gate_layouts(gl),
                            batch, n_ctx, seq, not last)
            fin = lambda hf, hb, xc, h, st: _ml_finish(
                hf, hb, xc, h, st, mod, wz, ml_norm_g[j][None], ml_skip[j][None], ml_w_down[j].astype(BF16),
                *ln(1), alpha)
            hl = fin(outs[0], outs[1], xcl, hl, lat)
            if not last:
                hc = fin(outs[2], outs[3], xcc, hc, con)
        elif kind == 1:
            w = _attn_weights(at_w_qkv[j])
            wo = at_w_o[j].astype(BF16)
            ql, kl, vl = _qkv(hl, lat, mod, w, _rope_tables(seq))
            qc, kc, vc = _qkv(hc, con, mod, w, None)
            hl = _attn(hl, lat, mod, at_sink[j], ql, kl, vl, kc, vc, wo, *ln(1), alpha, n_ctx, True)
            if not last:
                hc = _attn(hc, con, mod, at_sink[j], qc, None, None, kc, vc, wo, *ln(1), alpha, n_ctx, False)
        else:
            w = sc_w_in[j]
            wb, wcx, wo = w[:, :d].astype(BF16), w[:, d:].astype(BF16), sc_w_out[j].astype(BF16)
            hl = _sconv(hl, lat, mod, wb, wcx, sc_conv_w[j], wo, *ln(1), alpha)
            if not last:
                hc = _sconv(hc, con, mod, wb, wcx, sc_conv_w[j], wo, *ln(1), alpha)

        hl, hc = ffn_both(hl, hc, 1, not last)
    return hl.reshape(batch, seq, d)
```

```python
import functools

import jax
import jax.numpy as jnp
import numpy as np
from jax import lax
from jax.experimental import pallas as pl
from jax.experimental.pallas import tpu as pltpu

N_MIXERS = 3
N_MOD = 9
LN_EPS = 1e-5
GRID_W = 64
ML_HEADS = 4
ML_QKV_BLOCK = 4
AT_HEADS = 16
AT_KV_HEADS = 4
AT_HEAD_DIM = 64
AT_WINDOW = 128
ROPE_BASE = 10000.0
LOG2_E = 1.4426950408889634

V7X_LANES = 128
V7X_SUBLANES = 8
V7X_MXU_DIM = 256
V7X_VMEM_LIMIT_BYTES = 56 * 1024 * 1024

ROW_TILE = 256
WIDE_ROW_TILE = 512
ML_CHUNK = 256
ML_SCAN_HEADS = 4
FFN_SUB_TILES = 4
FINISH_SUB_TILES = 1
MOD_ROWS = 16
MOD_COL_TILE = 1024

BF16 = jnp.bfloat16
F32 = jnp.float32


def _cparams(*sem):
    return pltpu.CompilerParams(dimension_semantics=sem, vmem_limit_bytes=V7X_VMEM_LIMIT_BYTES)


def _resident(shape):
    nd = len(shape)
    return pl.BlockSpec(shape, lambda *_: (0,) * nd, pipeline_mode=pl.Buffered(1))


def _dot(a, b):
    return jnp.dot(a, b, preferred_element_type=F32)


def _sigmoid(x):
    return 1.0 / (1.0 + jnp.exp(-x))


def _silu(x):
    return x * _sigmoid(x)


def _modulate(h, mod_ref, slot):
    shift = mod_ref[pl.ds(3 * slot, 1), :]
    scale = mod_ref[pl.ds(3 * slot + 1, 1), :]
    return h * (1.0 + scale) + shift


def _post_norm(h, y, gate, weight, g, b, alpha):
    z = alpha * h + (weight * gate) * y
    mu = jnp.mean(z, axis=-1, keepdims=True)
    zc = z - mu
    var = jnp.mean(zc * zc, axis=-1, keepdims=True)
    return zc * lax.rsqrt(var + LN_EPS) * g + b


def _mod_kernel(cond_ref, w_ref, b_ref, o_ref):
    cond = _silu(cond_ref[...]).astype(BF16)
    o_ref[...] = _dot(cond, w_ref[...].astype(BF16)) + b_ref[...]


def _mod_all(cond, mod_w, mod_b):
    depth, d, n = mod_w.shape
    tn = MOD_COL_TILE
    return pl.pallas_call(
        _mod_kernel,
        grid=(depth, n // tn),
        in_specs=[
            pl.BlockSpec((MOD_ROWS, d), lambda i, j: (0, 0)),
            pl.BlockSpec((None, d, tn), lambda i, j: (i, 0, j)),
            pl.BlockSpec((None, 1, tn), lambda i, j: (i, 0, j)),
        ],
        out_specs=pl.BlockSpec((None, MOD_ROWS, tn), lambda i, j: (i, 0, j)),
        out_shape=jax.ShapeDtypeStruct((depth, MOD_ROWS, n), F32),
        compiler_params=_cparams("parallel", "parallel"),
        name="mod_vectors",
    )(cond, mod_w, mod_b.reshape(depth, 1, n))


class _Stream:
    def __init__(self, rows, seq, mod_base, rows_per_mod):
        self.rows = rows
        self.seq = seq
        self.mod_base = mod_base
        self.rows_per_mod = rows_per_mod

    def mod_spec(self, tm, d):
        base, rpm = self.mod_base, self.rows_per_mod
        return pl.BlockSpec((None, N_MOD, d), lambda i: (base + (i * tm) // rpm, 0, 0))


def _row_spec(tm, width):
    return pl.BlockSpec((tm, width), lambda i: (i, 0))


def _ffn_kernel(h_ref, mod_ref, wg_ref, wv_ref, wo_ref, lng_ref, lnb_ref, o_ref, *, slot, alpha):
    gate = mod_ref[pl.ds(3 * slot + 2, 1), :]
    sub = h_ref.shape[0] // FFN_SUB_TILES
    for r in range(FFN_SUB_TILES):
        rows = pl.ds(r * sub, sub)
        h = h_ref[rows, :]
        u = _modulate(h, mod_ref, slot).astype(BF16)
        g = _dot(u, wg_ref[...])
        v = _dot(u, wv_ref[...])
        a = (_silu(g) * v).astype(BF16)
        y = _dot(a, wo_ref[...])
        o_ref[rows, :] = _post_norm(h, y, gate, 0.5, lng_ref[...], lnb_ref[...], alpha)


def _ffn(h, st, mod, w_in, w_out, layer, half, lng, lnb, alpha, tm):
    rows, d = h.shape
    f = w_out.shape[2]
    slot = 2 * half

    def resident4(shape, col):
        return pl.BlockSpec((None, None) + shape, lambda i: (layer, half, 0, col), pipeline_mode=pl.Buffered(1))

    return pl.pallas_call(
        functools.partial(_ffn_kernel, slot=slot, alpha=alpha),
        grid=(rows // tm,),
        in_specs=[
            _row_spec(tm, d),
            st.mod_spec(tm, d),
            resident4((d, f), 0),
            resident4((d, f), 1),
            resident4((f, d), 0),
            _resident((1, d)),
            _resident((1, d)),
        ],
        out_specs=_row_spec(tm, d),
        out_shape=jax.ShapeDtypeStruct((rows, d), F32),
        compiler_params=_cparams("parallel"),
        name="ffn_swiglu",
    )(h, mod, w_in, w_in, w_out, lng, lnb)


def _halo_specs(tm, width, rows):
    r8 = tm // V7X_SUBLANES
    last8 = rows // V7X_SUBLANES - 1
    prev = pl.BlockSpec((V7X_SUBLANES, width), lambda i: (jnp.maximum(i * r8 - 1, 0), 0))
    nxt = pl.BlockSpec((V7X_SUBLANES, width), lambda i: (jnp.minimum((i + 1) * r8, last8), 0))
    return prev, nxt


def _halo_valid(tm, seq):
    tiles_per_seq = seq // tm
    pos = pl.program_id(0) % tiles_per_seq
    prev_ok = jnp.where(pos == 0, 0.0, 1.0).astype(F32)
    next_ok = jnp.where(pos == tiles_per_seq - 1, 0.0, 1.0).astype(F32)
    return prev_ok, next_ok


def _conv3(t_prev_row, t_main, t_next_row, w_ref, tm):
    row = lax.broadcasted_iota(jnp.int32, (tm, 1), 0)
    up = jnp.where(row == 0, t_prev_row, pltpu.roll(t_main, 1, axis=0))
    down = jnp.where(row == tm - 1, t_next_row, pltpu.roll(t_main, tm - 1, axis=0))
    return w_ref[0:1, :] * up + w_ref[1:2, :] * t_main + w_ref[2:3, :] * down


def _sconv_kernel(hp_ref, h_ref, hn_ref, mod_ref, wb_ref, wcx_ref, cw_ref, wo_ref, lng_ref, lnb_ref,
                  o_ref, *, alpha, tm, seq):
    d = h_ref.shape[1]
    h = h_ref[...]
    prev_ok, next_ok = _halo_valid(tm, seq)
    u_ext = _modulate(jnp.concatenate([hp_ref[...], h, hn_ref[...]], axis=0), mod_ref, 1).astype(BF16)
    lo, hi = V7X_SUBLANES, V7X_SUBLANES + tm
    bg = _dot(_modulate(h, mod_ref, 1).astype(BF16), wb_ref[...])
    cx = _dot(u_ext, wcx_ref[...])
    t_ext = cx[:, :d] * cx[:, d:]
    t_prev = t_ext[lo - 1:lo, :] * prev_ok
    t_next = t_ext[hi:hi + 1, :] * next_ok
    conv = _conv3(t_prev, t_ext[lo:hi], t_next, cw_ref, tm)
    y = _dot((bg * conv).astype(BF16), wo_ref[...])
    gate = mod_ref[pl.ds(5, 1), :]
    o_ref[...] = _post_norm(h, y, gate, 1.0, lng_ref[...], lnb_ref[...], alpha)


def _sconv(h, st, mod, wb, wcx, cw, wo, lng, lnb, alpha):
    rows, d = h.shape
    tm = min(WIDE_ROW_TILE, st.seq)
    prev, nxt = _halo_specs(tm, d, rows)
    return pl.pallas_call(
        functools.partial(_sconv_kernel, alpha=alpha, tm=tm, seq=st.seq),
        grid=(rows // tm,),
        in_specs=[
            prev, _row_spec(tm, d), nxt,
            st.mod_spec(tm, d),
            _resident(wb.shape), _resident(wcx.shape), _resident(cw.shape), _resident(wo.shape),
            _resident((1, d)), _resident((1, d)),
        ],
        out_specs=_row_spec(tm, d),
        out_shape=jax.ShapeDtypeStruct((rows, d), F32),
        compiler_params=_cparams("parallel"),
        name="short_conv_mixer",
    )(h, h, h, mod, wb, wcx, cw, wo, lng, lnb)


def _rope_tables(n_tokens):
    rows_n = n_tokens // GRID_W
    rows = jnp.repeat(jnp.arange(rows_n), GRID_W).astype(F32)
    cols = jnp.tile(jnp.arange(GRID_W), rows_n).astype(F32)
    axis_dim = AT_HEAD_DIM // 2
    freqs = ROPE_BASE ** (-jnp.arange(0, axis_dim, 2, dtype=F32) / axis_dim)
    ang = jnp.concatenate([rows[:, None] * freqs, cols[:, None] * freqs], axis=-1)
    cos, sin = jnp.cos(ang), jnp.sin(ang)
    cos_t = jnp.tile(jnp.concatenate([cos, cos], axis=-1), (1, 2))
    sin_t = jnp.tile(jnp.concatenate([-sin, sin], axis=-1), (1, 2))
    return cos_t, sin_t


def _swap_halves(x):
    width = x.shape[1]
    half = AT_HEAD_DIM // 2
    lane = lax.broadcasted_iota(jnp.int32, (1, width), 1)
    lo = (lane % AT_HEAD_DIM) < half
    return jnp.where(lo, pltpu.roll(x, width - half, axis=1), pltpu.roll(x, half, axis=1))


def _qkv_kernel(*refs, rope):
    if rope:
        h_ref, mod_ref, w_ref, cos_ref, sin_ref, q_ref, k_ref, v_ref = refs
    else:
        h_ref, mod_ref, w_ref, q_ref, k_ref, v_ref = refs
    nq, nk = q_ref.shape[1], k_ref.shape[1]
    u = _modulate(h_ref[...], mod_ref, 1).astype(BF16)
    p = _dot(u, w_ref[...])
    q = p[:, :nq] * (AT_HEAD_DIM ** -0.5 * LOG2_E)
    k = p[:, nq:nq + nk]
    if rope:
        cos, sin = cos_ref[...], sin_ref[...]
        q = q * jnp.tile(cos, (1, nq // V7X_LANES)) + _swap_halves(q) * jnp.tile(sin, (1, nq // V7X_LANES))
        k = k * jnp.tile(cos, (1, nk // V7X_LANES)) + _swap_halves(k) * jnp.tile(sin, (1, nk // V7X_LANES))
    q_ref[...] = q.astype(BF16)
    k_ref[...] = k.astype(BF16)
    v_ref[...] = p[:, nq + nk:].astype(BF16)


def _qkv(h, st, mod, w, tables):
    rows, d = h.shape
    tm = min(WIDE_ROW_TILE, st.seq)
    nq = AT_HEADS * AT_HEAD_DIM
    nk = AT_KV_HEADS * 2 * AT_HEAD_DIM
    rope = tables is not None
    in_specs = [_row_spec(tm, d), st.mod_spec(tm, d), _resident(w.shape)]
    args = [h, mod, w]
    if rope:
        tiles_per_seq = st.seq // tm
        tab_spec = pl.BlockSpec((tm, V7X_LANES), lambda i: (i % tiles_per_seq, 0))
        in_specs += [tab_spec, tab_spec]
        args += list(tables)
    return pl.pallas_call(
        functools.partial(_qkv_kernel, rope=rope),
        grid=(rows // tm,),
        in_specs=in_specs,
        out_specs=[_row_spec(tm, nq), _row_spec(tm, nk), _row_spec(tm, nk)],
        out_shape=[jax.ShapeDtypeStruct((rows, nq), BF16),
                   jax.ShapeDtypeStruct((rows, nk), BF16),
                   jax.ShapeDtypeStruct((rows, nk), BF16)],
        compiler_params=_cparams("parallel"),
        name="attn_qkv_rope",
    )(*args)


def _attn_kernel(*refs, window, tq, seq, alpha):
    if window:
        (sink_ref, q_ref, kp_ref, km_ref, kn_ref, kc_ref, vp_ref, vm_ref, vn_ref, vc_ref,
         h_ref, mod_ref, wo_ref, lng_ref, lnb_ref, o_ref) = refs
        k_parts = (kp_ref, km_ref, kn_ref, kc_ref)
        v_parts = (vp_ref, vm_ref, vn_ref, vc_ref)
    else:
        sink_ref, q_ref, kc_ref, vc_ref, h_ref, mod_ref, wo_ref, lng_ref, lnb_ref, o_ref = refs
        k_parts = (kc_ref,)
        v_parts = (vc_ref,)
    n_ctx = kc_ref.shape[0]
    lane = lax.broadcasted_iota(jnp.int32, (1, V7X_LANES), 1)
    lo = lane < AT_HEAD_DIM
    first_head = lax.broadcasted_iota(jnp.int32, (2 * tq, 1), 0) < tq

    if window:
        start = (pl.program_id(0) % (seq // tq)) * tq
        n_win = tq + 2 * AT_WINDOW
        q_pos = start + lax.broadcasted_iota(jnp.int32, (2 * tq, 1), 0) % tq
        k_idx = lax.broadcasted_iota(jnp.int32, (1, n_win + n_ctx), 1)
        k_pos = start - AT_WINDOW + k_idx
        valid = ((jnp.abs(q_pos - k_pos) <= AT_WINDOW) & (k_pos >= 0) & (k_pos < seq)) | (k_idx >= n_win)

    pairs = []
    for p in range(AT_HEADS // 2):
        grp = (2 * p) // (AT_HEADS // AT_KV_HEADS)
        cols = slice(grp * V7X_LANES, (grp + 1) * V7X_LANES)
        qp = q_ref[:, p * V7X_LANES:(p + 1) * V7X_LANES]
        zero = jnp.zeros_like(qp)
        qs = jnp.concatenate([jnp.where(lo, qp, zero), jnp.where(lo, zero, qp)], axis=0)
        kc = jnp.concatenate([r[:, cols] for r in k_parts], axis=0)
        vc = jnp.concatenate([r[:, cols] for r in v_parts], axis=0)
        s = lax.dot_general(qs, kc, (((1,), (1,)), ((), ())), preferred_element_type=F32)
        if window:
            s = jnp.where(valid, s, -jnp.inf)
        sk = jnp.where(first_head, sink_ref[2 * p], sink_ref[2 * p + 1]) * LOG2_E
        mx = jnp.maximum(jnp.max(s, axis=-1, keepdims=True), sk)
        e = jnp.exp2(s - mx)
        den = jnp.sum(e, axis=-1, keepdims=True) + jnp.exp2(sk - mx)
        o2 = _dot(e.astype(BF16), vc) * (1.0 / den)
        pairs.append(jnp.where(lo, o2[:tq], o2[tq:]))
    o = jnp.concatenate(pairs, axis=1).astype(BF16)
    y = _dot(o, wo_ref[...])
    h = h_ref[...]
    gate = mod_ref[pl.ds(5, 1), :]
    o_ref[...] = _post_norm(h, y, gate, 1.0, lng_ref[...], lnb_ref[...], alpha)


def _attn(h, st, mod, sink, q, k, v, kctx, vctx, wo, lng, lnb, alpha, n_ctx, window):
    rows, d = h.shape
    nq, nk = q.shape[1], kctx.shape[1]
    tq = ROW_TILE
    tiles_per_seq = st.seq // tq
    smem = pl.BlockSpec(memory_space=pltpu.SMEM)
    ctx_spec = pl.BlockSpec((n_ctx, nk), lambda i: (i // tiles_per_seq, 0))
    if window:
        w_blk = AT_WINDOW
        per_tile = tq // w_blk
        last = rows // w_blk - 1
        per_seq = st.seq // w_blk

        def prev_map(i):
            return (jnp.maximum(i * per_tile - 1, 0), 0)

        def next_map(i):
            return (jnp.minimum((i + 1) * per_tile, last), 0)

        kv_specs = [pl.BlockSpec((w_blk, nk), prev_map), _row_spec(tq, nk), pl.BlockSpec((w_blk, nk), next_map),
                    ctx_spec]
        in_specs = [smem, _row_spec(tq, nq)] + kv_specs + kv_specs
        args = [sink, q, k, k, k, kctx, v, v, v, vctx]
        del per_seq
    else:
        in_specs = [smem, _row_spec(tq, nq), ctx_spec, ctx_spec]
        args = [sink, q, kctx, vctx]
    in_specs += [_row_spec(tq, d), st.mod_spec(tq, d), _resident(wo.shape), _resident((1, d)), _resident((1, d))]
    args += [h, mod, wo, lng, lnb]
    return pl.pallas_call(
        functools.partial(_attn_kernel, window=window, tq=tq, seq=st.seq, alpha=alpha),
        grid=(rows // tq,),
        in_specs=in_specs,
        out_specs=_row_spec(tq, d),
        out_shape=jax.ShapeDtypeStruct((rows, d), F32),
        compiler_params=_cparams("parallel"),
        name="attn_window" if window else "attn_context",
    )(*args)


def _attn_weights(w_qkv):
    d = AT_HEAD_DIM
    deint = np.concatenate([np.arange(0, d, 2), np.arange(1, d, 2)])
    q_cols = np.concatenate([h * d + deint for h in range(AT_HEADS)])
    k0 = AT_HEADS * d
    k_cols = np.concatenate([np.tile(k0 + g * d + deint, 2) for g in range(AT_KV_HEADS)])
    v0 = k0 + AT_KV_HEADS * d
    v_cols = np.concatenate([np.tile(v0 + g * d + np.arange(d), 2) for g in range(AT_KV_HEADS)])
    return w_qkv[:, np.concatenate([q_cols, k_cols, v_cols])].astype(BF16)


def _ml_prep_kernel(hp_ref, h_ref, hn_ref, mod_ref, wxm_ref, cw_ref, cb_ref, bdqk_ref, bdv_ref,
                    wif_ref, bif_ref, q_ref, kt_ref, v_ref, xc_ref, g_ref, wfold_ref, *, tm, seq, k_scale):
    e = xc_ref.shape[1]
    dh = e // ML_HEADS
    blk = V7X_MXU_DIM

    @pl.when(pl.program_id(0) == 0)
    def _():
        for j in range(e // blk):
            rows = slice(j * blk, (j + 1) * blk)
            bd = bdqk_ref[j]
            fold_c = (_dot(bd[:, :blk], wif_ref[rows, :])
                      + _dot(bd[:, blk:], wif_ref[e + j * blk:e + (j + 1) * blk, :]))
            wfold_ref[rows, :] = fold_c.astype(BF16)
            fold_m = _dot(bdv_ref[j], wif_ref[2 * e + j * blk:2 * e + (j + 1) * blk, :])
            wfold_ref[e + j * blk:e + (j + 1) * blk, :] = fold_m.astype(BF16)

    prev_ok, next_ok = _halo_valid(tm, seq)
    u_ext = _modulate(jnp.concatenate([hp_ref[...], h_ref[...], hn_ref[...]], axis=0), mod_ref, 1).astype(BF16)
    lo, hi = V7X_SUBLANES, V7X_SUBLANES + tm
    xm_ext = _dot(u_ext, wxm_ref[...])
    xm = xm_ext[lo:hi]
    xm_prev = xm_ext[lo - 1:lo, :] * prev_ok
    xm_next = xm_ext[hi:hi + 1, :] * next_ok
    xc = _silu(_conv3(xm_prev, xm, xm_next, cw_ref, tm) + cb_ref[...])
    xcb = xc.astype(BF16)
    xc_ref[...] = xcb
    xmb = xm.astype(BF16)
    qs, ks, vs = [], [], []
    for j in range(e // blk):
        qk = _dot(xcb[:, j * blk:(j + 1) * blk], bdqk_ref[j])
        qs.append(qk[:, :blk])
        ks.append(qk[:, blk:])
        vs.append(_dot(xmb[:, j * blk:(j + 1) * blk], bdv_ref[j]))
    q = jnp.concatenate(qs, axis=1).astype(BF16)
    k = jnp.concatenate(ks, axis=1)
    v = jnp.concatenate(vs, axis=1).astype(BF16)
    gates = _dot(jnp.concatenate([xcb, xmb], axis=1), wfold_ref[...]) + bif_ref[...]
    for hd in range(ML_HEADS):
        q_ref[hd] = q[:, hd * dh:(hd + 1) * dh]
        v_ref[hd] = v[:, hd * dh:(hd + 1) * dh]
    kt_ref[...] = (k * k_scale).T.astype(BF16)

    n_h = ML_HEADS
    lf = _log_sigmoid(gates)
    lf_hi = lf.astype(BF16)
    lf_lo = (lf - lf_hi.astype(F32)).astype(BF16)
    ri = lax.broadcasted_iota(jnp.int32, (tm, tm), 0)
    ci = lax.broadcasted_iota(jnp.int32, (tm, tm), 1)
    lower = (ci <= ri).astype(BF16)
    prefix = _dot(lower, lf_hi) + _dot(lower, lf_lo)
    suffix = prefix[tm - 1:tm, :] - prefix + lf
    lane = lax.broadcasted_iota(jnp.int32, (1, V7X_LANES), 1)
    g_ref[...] = jnp.where((lane >= n_h) & (lane < 2 * n_h), prefix,
                           jnp.where((lane >= 3 * n_h) & (lane < 4 * n_h), suffix, gates))


def _ml_prep(h, st, mod, wxm, cw, cb, bdqk, bdv, wif, bif):
    rows, d = h.shape
    e = wxm.shape[1]
    dh = e // ML_HEADS
    tm = ML_CHUNK
    prev, nxt = _halo_specs(tm, d, rows)
    k_scale = dh ** -0.5
    heads = pl.BlockSpec((ML_HEADS, tm, dh), lambda i: (0, i, 0))
    heads_shape = jax.ShapeDtypeStruct((ML_HEADS, rows, dh), BF16)
    return pl.pallas_call(
        functools.partial(_ml_prep_kernel, tm=tm, seq=st.seq, k_scale=k_scale),
        grid=(rows // tm,),
        in_specs=[
            prev, _row_spec(tm, d), nxt, st.mod_spec(tm, d),
            _resident(wxm.shape), _resident(cw.shape), _resident(cb.shape),
            _resident(bdqk.shape), _resident(bdv.shape), _resident(wif.shape), _resident(bif.shape),
        ],
        out_specs=[heads, pl.BlockSpec((None, e, tm), lambda i: (i, 0, 0)), heads, _row_spec(tm, e),
                   _row_spec(tm, V7X_LANES)],
        out_shape=[heads_shape, jax.ShapeDtypeStruct((rows // tm, e, tm), BF16), heads_shape,
                   jax.ShapeDtypeStruct((rows, e), BF16), jax.ShapeDtypeStruct((rows, V7X_LANES), F32)],
        scratch_shapes=[pltpu.VMEM((2 * e, V7X_LANES), BF16)],
        compiler_params=_cparams("arbitrary"),
        name="mlstm_prep",
    )(h, h, h, mod, wxm, cw, cb, bdqk, bdv, wif, bif)


def _log_sigmoid(x):
    return jnp.minimum(x, 0.0) - jnp.log(1.0 + jnp.exp(-jnp.abs(x)))


def _ml_chunk(q_ref, kt_ref, v_ref, gr_ref, gc_ref, out_ref, c_ref, cb_ref, n_ref, m_ref, *, head, backward, t):
    q, kt, v = q_ref[...], kt_ref[...], v_ref[...]
    i_idx = 2 * ML_HEADS * int(backward) + head
    b_idx = i_idx + ML_HEADS
    ig_row, b_row = gr_ref[pl.ds(i_idx, 1), :], gr_ref[pl.ds(b_idx, 1), :]
    lane = lax.broadcasted_iota(jnp.int32, (1, V7X_LANES), 1)
    b_col = jnp.sum(jnp.where(lane == b_idx, gc_ref[...], 0.0), axis=-1, keepdims=True)
    m_prev = m_ref[0:1, 0:1]
    g = jnp.min(b_row, axis=-1, keepdims=True)

    if out_ref is not None:
        ri = lax.broadcasted_iota(jnp.int32, (t, t), 0)
        ci = lax.broadcasted_iota(jnp.int32, (t, t), 1)
        before = (ci >= ri) if backward else (ci <= ri)
        r = jnp.where(before, ig_row - b_row, -jnp.inf)
        mx = jnp.maximum(jnp.max(r, axis=-1, keepdims=True), m_prev)
        w = _dot(q, kt) * jnp.exp(r - mx)
        w_inter = jnp.exp(m_prev - mx)
        qn = jnp.sum(q.astype(F32) * n_ref[0:1, :], axis=-1, keepdims=True)
        den = jnp.sum(w, axis=-1, keepdims=True) + w_inter * qn
        scale = 1.0 / jnp.maximum(jnp.abs(den), jnp.exp(-(b_col + mx)))
        out_ref[...] = ((_dot(w.astype(BF16), v) + w_inter * _dot(q, cb_ref[...])) * scale).astype(out_ref.dtype)

    a_row = g - b_row + ig_row
    m_new = jnp.maximum(g + m_prev, jnp.max(a_row, axis=-1, keepdims=True))
    w_s = jnp.exp(a_row - m_new).astype(BF16)
    decay = jnp.exp(g + m_prev - m_new)
    c_new = decay * c_ref[...] + _dot(kt * w_s, v)
    c_ref[...] = c_new
    cb_ref[...] = c_new.astype(BF16)
    w_s8 = jnp.broadcast_to(w_s, (V7X_SUBLANES * 2, t))
    n_ref[...] = decay * n_ref[...] + lax.dot_general(
        w_s8, kt, (((1,), (1,)), ((), ())), preferred_element_type=F32)
    m_ref[...] = jnp.broadcast_to(m_new, m_ref.shape)


def _ml_scan_kernel(*refs, ctx_out, t, heads):
    ctx_in, fwd_in, bwd_in = refs[0:5], refs[5:10], refs[10:15]
    n_out = 4 if ctx_out else 2
    outs = refs[15:15 + n_out]
    c_ref, cb_ref, n_ref, m_ref = refs[15 + n_out:]
    hf_ref, hb_ref = outs[0], outs[1]
    hcf_ref, hcb_ref = (outs[2], outs[3]) if ctx_out else (None, None)
    step = pl.program_id(2)
    dh = c_ref.shape[1]

    def run(inputs, out_ref, backward):
        q_ref, kt_ref, v_ref, gr_ref, gc_ref = inputs
        for hh in range(heads):
            slot = heads * int(backward) + hh
            head = pl.program_id(1) * heads + hh
            _ml_chunk(q_ref.at[hh], kt_ref.at[pl.ds(hh * dh, dh), :], v_ref.at[hh], gr_ref, gc_ref,
                      None if out_ref is None else out_ref.at[hh],
                      c_ref.at[slot], cb_ref.at[slot], n_ref.at[slot], m_ref.at[slot],
                      head=head, backward=backward, t=t)

    @pl.when(step == 0)
    def _():
        for ref in (c_ref, cb_ref, n_ref, m_ref):
            ref[...] = jnp.zeros_like(ref)
        run(ctx_in, hcf_ref, False)
        run(ctx_in, hcb_ref, True)

    @pl.when(step > 0)
    def _():
        run(fwd_in, hf_ref, False)
        run(bwd_in, hb_ref, True)


def _ml_scan(qkv_c, gates_c, qkv_l, gates_l, batch, n_ctx, seq, ctx_out):
    qc, ktc, vc = qkv_c
    ql, ktl, vl = qkv_l
    grc, gcc = gates_c
    grl, gcl = gates_l
    n_heads, _, dh = ql.shape
    t = ML_CHUNK
    heads = ML_SCAN_HEADS
    assert n_ctx == t, "context length must equal the mLSTM chunk length"
    ncl = seq // t
    n_gates = grl.shape[0]
    fwd = lambda b, s: b * ncl + jnp.maximum(s - 1, 0)
    bwd = lambda b, s: b * ncl + ncl - 1 - jnp.maximum(s - 1, 0)
    ctx = lambda b, s: b

    def head_rows(chunk):
        return pl.BlockSpec((heads, t, dh), lambda b, hg, s: (hg, chunk(b, s), 0))

    def chunk_specs(chunk):
        return [head_rows(chunk),
                pl.BlockSpec((None, heads * dh, t), lambda b, hg, s: (chunk(b, s), hg, 0)),
                head_rows(chunk),
                pl.BlockSpec((n_gates, t), lambda b, hg, s: (0, chunk(b, s))),
                pl.BlockSpec((t, V7X_LANES), lambda b, hg, s: (chunk(b, s), 0))]

    out_specs = [head_rows(fwd), head_rows(bwd)]
    out_shape = [jax.ShapeDtypeStruct((n_heads, batch * seq, dh), BF16)] * 2
    if ctx_out:
        out_specs += [head_rows(ctx), head_rows(ctx)]
        out_shape += [jax.ShapeDtypeStruct((n_heads, batch * n_ctx, dh), BF16)] * 2
    slots = 2 * heads
    return pl.pallas_call(
        functools.partial(_ml_scan_kernel, ctx_out=ctx_out, t=t, heads=heads),
        grid=(batch, n_heads // heads, ncl + 1),
        in_specs=chunk_specs(ctx) + chunk_specs(fwd) + chunk_specs(bwd),
        out_specs=out_specs,
        out_shape=out_shape,
        scratch_shapes=[pltpu.VMEM((slots, dh, dh), F32), pltpu.VMEM((slots, dh, dh), BF16),
                        pltpu.VMEM((slots, 2 * V7X_SUBLANES, dh), F32),
                        pltpu.VMEM((slots, V7X_SUBLANES, V7X_LANES), F32)],
        compiler_params=_cparams("parallel", "parallel", "arbitrary"),
        name="mlstm_scan",
    )(qc, ktc, vc, grc, gcc, ql, ktl, vl, grl, gcl, ql, ktl, vl, grl, gcl)


def _ml_finish_kernel(hf_ref, hb_ref, xc_ref, h_ref, mod_ref, wz_ref, ng_ref, skip_ref, wd_ref, lng_ref, lnb_ref,
                      o_ref, *, alpha):
    dh = hf_ref.shape[2]
    gate = mod_ref[pl.ds(5, 1), :]
    sub = h_ref.shape[0] // FINISH_SUB_TILES
    for r in range(FINISH_SUB_TILES):
        rows = pl.ds(r * sub, sub)
        h = h_ref[rows, :]
        z = _dot(_modulate(h, mod_ref, 1).astype(BF16), wz_ref[...])
        parts = []
        for hd in range(ML_HEADS):
            hsum = hf_ref[hd, rows, :].astype(F32) + hb_ref[hd, rows, :].astype(F32)
            x = hsum * _sigmoid(z[:, hd * dh:(hd + 1) * dh])
            mu = jnp.mean(x, axis=-1, keepdims=True)
            xz = x - mu
            var = jnp.mean(xz * xz, axis=-1, keepdims=True)
            parts.append(xz * lax.rsqrt(var + LN_EPS))
        hn = jnp.concatenate(parts, axis=1) * ng_ref[...]
        y = _dot((hn + skip_ref[...] * xc_ref[rows, :].astype(F32)).astype(BF16), wd_ref[...])
        o_ref[rows, :] = _post_norm(h, y, gate, 1.0, lng_ref[...], lnb_ref[...], alpha)


def _ml_finish(hf, hb, xc, h, st, mod, wz, ng, skip, wd, lng, lnb, alpha):
    rows, d = h.shape
    n_heads, _, dh = hf.shape
    e = n_heads * dh
    tm = WIDE_ROW_TILE
    heads = pl.BlockSpec((n_heads, tm, dh), lambda i: (0, i, 0))
    return pl.pallas_call(
        functools.partial(_ml_finish_kernel, alpha=alpha),
        grid=(rows // tm,),
        in_specs=[
            heads, heads, _row_spec(tm, e), _row_spec(tm, d), st.mod_spec(tm, d), _resident(wz.shape),
            _resident((1, e)), _resident((1, e)), _resident(wd.shape), _resident((1, d)), _resident((1, d)),
        ],
        out_specs=_row_spec(tm, d),
        out_shape=jax.ShapeDtypeStruct((rows, d), F32),
        compiler_params=_cparams("parallel"),
        name="mlstm_finish",
    )(hf, hb, xc, h, mod, wz, ng, skip, wd, lng, lnb)


def _blockdiag_tiles(w):
    nblk, bs, _ = w.shape
    side = V7X_MXU_DIM
    rows = w.reshape(nblk * bs // side, side, bs)
    idx = np.arange(side) // bs
    on_diagonal = jnp.asarray(idx[:, None] == idx[None, :])
    return jnp.where(on_diagonal, jnp.tile(rows, (1, 1, side // bs)), 0.0)


def _gate_layouts(g):
    return g[:, :4 * ML_HEADS].T, g


def kernel(x, c, ctx, c_ctx, mod_w, mod_b, ln_g, ln_b, ffn_w_in, ffn_w_out, ml_w_up, ml_conv_w, ml_conv_b,
           ml_w_qkv, ml_w_if, ml_b_if, ml_skip, ml_norm_g, ml_w_down, at_w_qkv, at_sink, at_w_o, sc_w_in,
           sc_conv_w, sc_w_out):
    batch, seq, d = x.shape
    n_ctx = ctx.shape[1]
    depth = mod_w.shape[0]
    alpha = (2 * depth) ** 0.25
    assert batch + 1 <= MOD_ROWS and seq % ROW_TILE == 0 and n_ctx % ROW_TILE == 0

    lat = _Stream(batch * seq, seq, 0, seq)
    con = _Stream(batch * n_ctx, n_ctx, batch, batch * n_ctx)
    hl = x.reshape(batch * seq, d)
    hc = ctx.reshape(batch * n_ctx, d)

    cond = jnp.zeros((MOD_ROWS, d), F32).at[:batch].set(c).at[batch].set(c_ctx)
    mods = _mod_all(cond, mod_w, mod_b).reshape(depth, MOD_ROWS, N_MOD, d)

    ffn_tm = WIDE_ROW_TILE
    ffn_in, ffn_out = ffn_w_in.astype(BF16), ffn_w_out.astype(BF16)
    for i in range(depth):
        kind, j, last = i % N_MIXERS, i // N_MIXERS, i == depth - 1
        mod = mods[i]
        ln = lambda s: (ln_g[i, s][None], ln_b[i, s][None])

        def ffn_both(hl, hc, s, do_ctx):
            hl = _ffn(hl, lat, mod, ffn_in, ffn_out, i, s, *ln(2 * s), alpha, ffn_tm)
            if do_ctx:
                hc = _ffn(hc, con, mod, ffn_in, ffn_out, i, s, *ln(2 * s), alpha, ffn_tm)
            return hl, hc

        hl, hc = ffn_both(hl, hc, 0, True)

        if kind == 0:
            e = ml_w_up.shape[2] // 2
            wxm, wz = ml_w_up[j][:, :e].astype(BF16), ml_w_up[j][:, e:].astype(BF16)
            bdq, bdk, bdv = (_blockdiag_tiles(ml_w_qkv[j, a]).astype(BF16) for a in range(3))
            bdqk = jnp.concatenate([bdq, bdk], axis=2)
            n_g = 4 * ML_HEADS
            wif = jnp.concatenate([ml_w_if[j, 0], ml_w_if[j, 1]], axis=1)
            wif = jnp.pad(wif, ((0, 0), (0, V7X_LANES - n_g))).astype(BF16)
            bif = jnp.pad(jnp.concatenate([ml_b_if[j, 0], ml_b_if[j, 1]]), (0, V7X_LANES - n_g))[None]
            prep = lambda h, st: _ml_prep(h, st, mod, wxm, ml_conv_w[j], ml_conv_b[j][None], bdqk, bdv, wif, bif)
            qc, kc, vc, xcc, gc = prep(hc, con)
            ql, kl, vl, xcl, gl = prep(hl, lat)
            outs = _ml_scan((qc, kc, vc), _gate_layouts(gc), (ql, kl, vl), _gate_layouts(gl),
                            batch, n_ctx, seq, not last)
            fin = lambda hf, hb, xc, h, st: _ml_finish(
                hf, hb, xc, h, st, mod, wz, ml_norm_g[j][None], ml_skip[j][None], ml_w_down[j].astype(BF16),
                *ln(1), alpha)
            hl = fin(outs[0], outs[1], xcl, hl, lat)
            if not last:
                hc = fin(outs[2], outs[3], xcc, hc, con)
        elif kind == 1:
            w = _attn_weights(at_w_qkv[j])
            wo = at_w_o[j].astype(BF16)
            ql, kl, vl = _qkv(hl, lat, mod, w, _rope_tables(seq))
            qc, kc, vc = _qkv(hc, con, mod, w, None)
            hl = _attn(hl, lat, mod, at_sink[j], ql, kl, vl, kc, vc, wo, *ln(1), alpha, n_ctx, True)
            if not last:
                hc = _attn(hc, con, mod, at_sink[j], qc, None, None, kc, vc, wo, *ln(1), alpha, n_ctx, False)
        else:
            w = sc_w_in[j]
            wb, wcx, wo = w[:, :d].astype(BF16), w[:, d:].astype(BF16), sc_w_out[j].astype(BF16)
            hl = _sconv(hl, lat, mod, wb, wcx, sc_conv_w[j], wo, *ln(1), alpha)
            if not last:
                hc = _sconv(hc, con, mod, wb, wcx, sc_conv_w[j], wo, *ln(1), alpha)

        hl, hc = ffn_both(hl, hc, 1, not last)
    return hl.reshape(batch, seq, d)
```
